```python
import math
import jax, jax.numpy as jnp
from jax import lax
import numpy as np

D_MODEL = 1024
BATCH = 8
SEQ = 4096
DEPTH = 2

CTX_LEN = 256
GRID_W = 64
Q_BLOCK = 128
ROPE_THETA = 10000.0
EPS = 1e-6

A_HEAD_DIM = 128
A_HEADS = D_MODEL // (2 * A_HEAD_DIM)
A_KV_HEADS = A_HEADS // 2
A_WIDTH = A_HEADS * A_HEAD_DIM
A_KV_WIDTH = A_KV_HEADS * A_HEAD_DIM
B_HEAD_DIM = 64
B_HEADS = D_MODEL // (4 * B_HEAD_DIM)
B_QK_WIDTH = B_HEADS * 2 * B_HEAD_DIM
B_WIDTH = B_HEADS * 2 * B_HEAD_DIM
ATTN_WIDTH = A_WIDTH + B_WIDTH
KV_COLS = 2 * A_KV_WIDTH + B_QK_WIDTH + B_WIDTH
ATTN_IN_COLS = KV_COLS + A_WIDTH + B_QK_WIDTH + ATTN_WIDTH
F_GROUPS = 4
F_WIDTH = D_MODEL
F_GROUP_DIM = F_WIDTH // F_GROUPS

N_ATTN_LAYERS = (DEPTH + 1) // 2
N_FOURIER_LAYERS = DEPTH // 2

kernel_name = "hybrid_gqa_diffattn_fourier_dit"


def rms_norm(x, g):
    xf = x.astype(jnp.float32)
    y = xf * lax.rsqrt(jnp.mean(xf * xf, axis=-1, keepdims=True) + EPS)
    return (y * g.astype(jnp.float32)).astype(x.dtype)


def modulate(x, g, shift, scale):
    return rms_norm(x, g) * (1 + scale) + shift


def ada_params(cond, w, b):
    m = jax.nn.silu(cond) @ w + b
    return jnp.split(m, 3, axis=-1)


def axial_rope_tables(n_tokens, dim, dtype):
    rows_count = n_tokens // GRID_W
    rows = jnp.repeat(jnp.arange(rows_count, dtype=jnp.float32), GRID_W)
    cols = jnp.tile(jnp.arange(GRID_W, dtype=jnp.float32), rows_count)
    quarter = dim // 4
    inv = ROPE_THETA ** (-jnp.arange(quarter, dtype=jnp.float32) / quarter)
    ang = jnp.stack([rows[:, None] * inv, cols[:, None] * inv], axis=1)
    return jnp.cos(ang).astype(dtype), jnp.sin(ang).astype(dtype)


def apply_axial_rope(x, cos, sin):
    b, n, h, dim = x.shape
    q4 = dim // 4
    xs = x.reshape(b, n, h, 2, 2, q4)
    x1, x2 = xs[..., 0, :], xs[..., 1, :]
    c = cos[None, :, None]
    s = sin[None, :, None]
    out = jnp.stack([x1 * c - x2 * s, x1 * s + x2 * c], axis=-2)
    return out.reshape(x.shape)


def sweep_query_blocks(fn, *qs):
    b, n = qs[0].shape[:2]
    nb = n // Q_BLOCK
    blocks = tuple(jnp.moveaxis(q.reshape(b, nb, Q_BLOCK, *q.shape[2:]), 1, 0) for q in qs)
    out = lax.map(lambda blk: fn(*blk), blocks)
    return jnp.moveaxis(out, 0, 1).reshape(b, n, *out.shape[3:])


def gqa_block(q, k, v):
    b, nq = q.shape[:2]
    qg = q.reshape(b, nq, A_KV_HEADS, A_HEADS // A_KV_HEADS, A_HEAD_DIM)
    s = jnp.einsum('bqkgd,blkd->bkgql', qg, k).astype(jnp.float32) * (A_HEAD_DIM ** -0.5)
    p = jax.nn.softmax(s, axis=-1).astype(v.dtype)
    o = jnp.einsum('bkgql,blkd->bqkgd', p, v)
    return o.reshape(b, nq, A_WIDTH)


def diff_block(q1, q2, k1, k2, v, lam, subln_g, lam_init):
    b, nq = q1.shape[:2]
    scale = B_HEAD_DIM ** -0.5
    p1 = jax.nn.softmax(jnp.einsum('bqhd,blhd->bhql', q1, k1).astype(jnp.float32) * scale, axis=-1)
    p2 = jax.nn.softmax(jnp.einsum('bqhd,blhd->bhql', q2, k2).astype(jnp.float32) * scale, axis=-1)
    p = (p1 - lam * p2).astype(v.dtype)
    o = jnp.einsum('bhql,blhe->bqhe', p, v)
    o = rms_norm(o, subln_g) * (1.0 - lam_init)
    return o.reshape(b, nq, B_WIDTH)


def split_kv(p, kn_g):
    b, n = p.shape[:2]
    kA, vA, kB, vB = jnp.split(p, [A_KV_WIDTH, 2 * A_KV_WIDTH, 2 * A_KV_WIDTH + B_QK_WIDTH], axis=-1)
    kA = rms_norm(kA.reshape(b, n, A_KV_HEADS, A_HEAD_DIM), kn_g)
    vA = vA.reshape(b, n, A_KV_HEADS, A_HEAD_DIM)
    kB = kB.reshape(b, n, B_HEADS, 2, B_HEAD_DIM)
    vB = vB.reshape(b, n, B_HEADS, 2 * B_HEAD_DIM)
    return kA, vA, kB[..., 0, :], kB[..., 1, :], vB


def split_q(p, qn_g):
    b, n = p.shape[:2]
    qA, qB, gate = jnp.split(p, [A_WIDTH, A_WIDTH + B_QK_WIDTH], axis=-1)
    qA = rms_norm(qA.reshape(b, n, A_HEADS, A_HEAD_DIM), qn_g)
    qB = qB.reshape(b, n, B_HEADS, 2, B_HEAD_DIM)
    return qA, qB[..., 0, :], qB[..., 1, :], gate


def attention_layer(x, ctx, mods_x, mods_c, norm_g, w_in, qn_g, kn_g,
                    lam_q1, lam_k1, lam_q2, lam_k2, subln_g, w_out, lam_init, update_ctx):
    shift_x, scale_x, gate_x = mods_x
    shift_c, scale_c, gate_c = mods_c
    hx = modulate(x, norm_g, shift_x, scale_x)
    hc = modulate(ctx, norm_g, shift_c, scale_c)
    lam = (jnp.exp(jnp.sum(lam_q1.astype(jnp.float32) * lam_k1.astype(jnp.float32)))
           - jnp.exp(jnp.sum(lam_q2.astype(jnp.float32) * lam_k2.astype(jnp.float32)))
           + lam_init)

    pc = hc @ (w_in if update_ctx else w_in[:, :KV_COLS])
    kA_c, vA_c, k1_c, k2_c, vB_c = split_kv(pc[..., :KV_COLS], kn_g)

    px = hx @ w_in
    kA_x, vA_x, k1_x, k2_x, vB_x = split_kv(px[..., :KV_COLS], kn_g)
    qA_x, q1_x, q2_x, g_x = split_q(px[..., KV_COLS:], qn_g)
    n = x.shape[1]
    cos_a, sin_a = axial_rope_tables(n, A_HEAD_DIM, x.dtype)
    cos_b, sin_b = axial_rope_tables(n, B_HEAD_DIM, x.dtype)
    qA_x, kA_x = apply_axial_rope(qA_x, cos_a, sin_a), apply_axial_rope(kA_x, cos_a, sin_a)
    q1_x, k1_x = apply_axial_rope(q1_x, cos_b, sin_b), apply_axial_rope(k1_x, cos_b, sin_b)
    q2_x, k2_x = apply_axial_rope(q2_x, cos_b, sin_b), apply_axial_rope(k2_x, cos_b, sin_b)

    def mix(qA, q1, q2, kA, vA, k1, k2, vB):
        def blk(a, b1, b2):
            return jnp.concatenate([gqa_block(a, kA, vA),
                                    diff_block(b1, b2, k1, k2, vB, lam, subln_g, lam_init)], axis=-1)
        return sweep_query_blocks(blk, qA, q1, q2)

    cat = lambda a, b: jnp.concatenate([a, b], axis=1)
    ox = mix(qA_x, q1_x, q2_x, cat(kA_c, kA_x), cat(vA_c, vA_x),
             cat(k1_c, k1_x), cat(k2_c, k2_x), cat(vB_c, vB_x))
    x = x + gate_x * ((ox * jax.nn.silu(g_x)) @ w_out)

    if update_ctx:
        qA_c, q1_c, q2_c, g_c = split_q(pc[..., KV_COLS:], qn_g)
        oc = mix(qA_c, q1_c, q2_c, kA_c, vA_c, k1_c, k2_c, vB_c)
        ctx = ctx + gate_c * ((oc * jax.nn.silu(g_c)) @ w_out)
    return x, ctx


def fourier_layer(x, shift, scale, gate, norm_g, w_in, w_out):
    h = modulate(x, norm_g, shift, scale)
    u, g = jnp.split(h @ w_in, 2, axis=-1)
    b, n = u.shape[:2]
    uf = u.reshape(b, n, F_GROUPS, F_GROUP_DIM).astype(jnp.float32)
    f = jnp.fft.fft2(uf, axes=(1, 3), norm="ortho").real.astype(x.dtype).reshape(b, n, F_WIDTH)
    return x + gate * ((f * jax.nn.silu(g)) @ w_out)


def setup_inputs(seed: int = 0) -> dict:
    key = jax.random.key(seed)
    ks = jax.random.split(key, 24)
    nrm = lambda k, shape: jax.random.normal(k, shape, dtype=jnp.float32)
    D = D_MODEL
    return {
        "x": nrm(ks[0], (BATCH, SEQ, D)),
        "c": nrm(ks[1], (BATCH, D)),
        "ctx": nrm(ks[2], (BATCH, CTX_LEN, D)),
        "c_ctx": nrm(ks[3], (D,)),
        "ada_w": nrm(ks[4], (DEPTH, D, 3 * D)) * (0.5 * D ** -0.5),
        "ada_b": nrm(ks[5], (DEPTH, 3 * D)) * 0.01,
        "norm_g": 1.0 + 0.02 * nrm(ks[6], (DEPTH, D)),
        "attn_in_w": nrm(ks[7], (N_ATTN_LAYERS, D, ATTN_IN_COLS)) * D ** -0.5,
        "attn_qn_g": 1.0 + 0.02 * nrm(ks[8], (N_ATTN_LAYERS, A_HEAD_DIM)),
        "attn_kn_g": 1.0 + 0.02 * nrm(ks[9], (N_ATTN_LAYERS, A_HEAD_DIM)),
        "lam_q1": 0.1 * nrm(ks[10], (N_ATTN_LAYERS, B_HEAD_DIM)),
        "lam_k1": 0.1 * nrm(ks[11], (N_ATTN_LAYERS, B_HEAD_DIM)),
        "lam_q2": 0.1 * nrm(ks[12], (N_ATTN_LAYERS, B_HEAD_DIM)),
        "lam_k2": 0.1 * nrm(ks[13], (N_ATTN_LAYERS, B_HEAD_DIM)),
        "attn_subln_g": 1.0 + 0.02 * nrm(ks[14], (N_ATTN_LAYERS, 2 * B_HEAD_DIM)),
        "attn_out_w": nrm(ks[15], (N_ATTN_LAYERS, ATTN_WIDTH, D)) * ATTN_WIDTH ** -0.5,
        "fourier_in_w": nrm(ks[16], (N_FOURIER_LAYERS, D, 2 * F_WIDTH)) * D ** -0.5,
        "fourier_out_w": nrm(ks[17], (N_FOURIER_LAYERS, F_WIDTH, D)) * F_WIDTH ** -0.5,
        "final_g": 1.0 + 0.02 * nrm(ks[18], (D,)),
    }


def reference(x, c, ctx, c_ctx, ada_w, ada_b, norm_g, attn_in_w, attn_qn_g, attn_kn_g,
              lam_q1, lam_k1, lam_q2, lam_k2, attn_subln_g, attn_out_w,
              fourier_in_w, fourier_out_w, final_g):
    for l in range(DEPTH):
        i = l // 2
        update_ctx = any(j % 2 == 0 for j in range(l + 1, DEPTH))
        mods_x = [m[:, None, :] for m in ada_params(c, ada_w[l], ada_b[l])]
        if l % 2 == 0:
            mods_c = ada_params(c_ctx, ada_w[l], ada_b[l])
            lam_init = 0.8 - 0.6 * math.exp(-0.3 * l)
            x, ctx = attention_layer(x, ctx, mods_x, mods_c, norm_g[l], attn_in_w[i],
                                     attn_qn_g[i], attn_kn_g[i], lam_q1[i], lam_k1[i],
                                     lam_q2[i], lam_k2[i], attn_subln_g[i], attn_out_w[i],
                                     lam_init, update_ctx)
        else:
            x = fourier_layer(x, *mods_x, norm_g[l], fourier_in_w[i], fourier_out_w[i])
            if update_ctx:
                mods_c = ada_params(c_ctx, ada_w[l], ada_b[l])
                ctx = fourier_layer(ctx, *mods_c, norm_g[l], fourier_in_w[i], fourier_out_w[i])
    return rms_norm(x, final_g)
```

```python
import functools
import math

import numpy as np
import jax
import jax.numpy as jnp
from jax import lax
from jax.experimental import pallas as pl
from jax.experimental.pallas import tpu as pltpu

F32 = jnp.float32
BF16 = jnp.bfloat16

D = 1024
BATCH = 8
SEQ = 4096
CTX = 256
KEYS = CTX + SEQ
GRID_W = 64
ROPE_THETA = 10000.0
EPS = 1e-6

HD = 128
A_HEADS = 4
A_KV_HEADS = 2
B_HEADS = 4
B_HEAD_DIM = 64
KA0, VA0, KB0, VB0 = 0, 256, 512, 1024
KV_COLS = 1536
QA0, QB0, G0 = 1536, 2048, 2560
IN_COLS = 3584
F_GROUPS = 4
F_GROUP_DIM = 256

ROWS = 256
TQ = 256
TK = 256
VMEM_LIMIT = 56 * 1024 * 1024


def _silu(v):
    return v * (1.0 / (1.0 + jnp.exp(-v)))


def _rms_rows(v):
    return v * lax.rsqrt(jnp.mean(v * v, axis=-1, keepdims=True) + EPS)


def _ada_kernel(c_ref, w_ref, b_ref, o_ref):
    s = _silu(c_ref[...]).astype(BF16)
    o_ref[0] = jnp.dot(s, w_ref[0].astype(BF16), preferred_element_type=F32) + b_ref[0]


def _ada(cond, ada_w, ada_b):
    depth, _, n3 = ada_w.shape
    tn = 768
    return pl.pallas_call(
        _ada_kernel,
        grid=(depth, n3 // tn),
        in_specs=[
            pl.BlockSpec((16, D), lambda l, j: (0, 0)),
            pl.BlockSpec((1, D, tn), lambda l, j: (l, 0, j)),
            pl.BlockSpec((1, 1, tn), lambda l, j: (l, 0, j)),
        ],
        out_specs=pl.BlockSpec((1, 16, tn), lambda l, j: (l, 0, j)),
        out_shape=jax.ShapeDtypeStruct((depth, 16, n3), F32),
        compiler_params=pltpu.CompilerParams(
            dimension_semantics=("arbitrary", "arbitrary"), vmem_limit_bytes=VMEM_LIMIT),
        name="ada",
    )(cond, ada_w, ada_b.reshape(depth, 1, n3))


def _rope_table(dim):
    q4 = dim // 4
    pos = np.arange(SEQ)
    axis_pos = np.stack([pos // GRID_W, pos % GRID_W], axis=1).astype(np.float64)
    inv = ROPE_THETA ** (-np.arange(q4, dtype=np.float64) / q4)
    ang = axis_pos[:, :, None, None] * inv[None, None, None, :]
    ang = np.broadcast_to(ang, (SEQ, 2, 2, q4))
    sign = np.array([-1.0, 1.0]).reshape(1, 1, 2, 1)
    cos = np.cos(ang).reshape(SEQ, dim)
    sin = (np.sin(ang) * sign).reshape(SEQ, dim)
    return cos.astype(np.float32), sin.astype(np.float32)


_COS_A, _SIN_A = _rope_table(HD)
_cb, _sb = _rope_table(B_HEAD_DIM)
_COS_B, _SIN_B = np.tile(_cb, (1, 2)), np.tile(_sb, (1, 2))


def _rope(y, cos, sin, half):
    lane = lax.broadcasted_iota(jnp.int32, y.shape, 1)
    first = (lane & (2 * half - 1)) < half
    partner = jnp.where(first, pltpu.roll(y, HD - half, 1), pltpu.roll(y, half, 1))
    return y * cos + partner * sin


def _inproj_kernel(x_ref, ctx_ref, modx_ref, modc_ref, g_ref, w_ref, qn_ref, kn_ref,
                   cosa_ref, sina_ref, cosb_ref, sinb_ref, kv_ref, q_ref, gate_ref):
    t = pl.program_id(1)

    def hidden(src, mod):
        a = g_ref[...] * (1.0 + mod[1:2, :])
        return (_rms_rows(src) * a + mod[0:1, :]).astype(BF16)

    def proj(hb, c0, width):
        return jnp.dot(hb, w_ref[:, c0:c0 + width], preferred_element_type=F32)

    def keys_values(hb, rope):
        ka = proj(hb, KA0, 2 * HD)
        for h in range(A_KV_HEADS):
            y = _rms_rows(ka[:, h * HD:(h + 1) * HD]) * kn_ref[...]
            if rope:
                y = _rope(y, cosa_ref[...], sina_ref[...], HD // 4)
            kv_ref[0, :, KA0 + h * HD:KA0 + (h + 1) * HD] = y.astype(BF16)
        kv_ref[0, :, VA0:VA0 + 2 * HD] = proj(hb, VA0, 2 * HD).astype(BF16)
        kb = proj(hb, KB0, B_HEADS * HD)
        for h in range(B_HEADS):
            y = kb[:, h * HD:(h + 1) * HD]
            if rope:
                y = _rope(y, cosb_ref[...], sinb_ref[...], B_HEAD_DIM // 4)
            kv_ref[0, :, KB0 + h * HD:KB0 + (h + 1) * HD] = y.astype(BF16)
        kv_ref[0, :, VB0:VB0 + B_HEADS * HD] = proj(hb, VB0, B_HEADS * HD).astype(BF16)

    @pl.when(t == 0)
    def _():
        keys_values(hidden(ctx_ref[0], modc_ref[0]), rope=False)

    @pl.when(t > 0)
    def _():
        hb = hidden(x_ref[0], modx_ref[0])
        keys_values(hb, rope=True)
        qa = proj(hb, QA0, A_HEADS * HD)
        qgain = qn_ref[...] * (HD ** -0.5)
        for h in range(A_HEADS):
            y = _rms_rows(qa[:, h * HD:(h + 1) * HD]) * qgain
            y = _rope(y, cosa_ref[...], sina_ref[...], HD // 4)
            q_ref[0, :, h * HD:(h + 1) * HD] = y.astype(BF16)
        qb = proj(hb, QB0, B_HEADS * HD)
        for h in range(B_HEADS):
            y = _rope(qb[:, h * HD:(h + 1) * HD], cosb_ref[...], sinb_ref[...], B_HEAD_DIM // 4)
            q_ref[0, :, (A_HEADS + h) * HD:(A_HEADS + h + 1) * HD] = (y * (B_HEAD_DIM ** -0.5)).astype(BF16)
        gate_ref[0] = _silu(proj(hb, G0, D)).astype(BF16)


def _inproj(x, ctx, mod, norm_g, w_in, qn_g, kn_g):
    nt = KEYS // ROWS
    xrow = lambda b, t: (b, jnp.maximum(t - 1, 0), 0)
    trow = lambda b, t: (jnp.maximum(t - 1, 0), 0)
    const2 = lambda b, t: (0, 0)
    table = pl.BlockSpec((ROWS, HD), trow)
    return pl.pallas_call(
        _inproj_kernel,
        grid=(BATCH, nt),
        in_specs=[
            pl.BlockSpec((1, ROWS, D), xrow),
            pl.BlockSpec((1, CTX, D), lambda b, t: (b, 0, 0)),
            pl.BlockSpec((1, 3, D), lambda b, t: (b, 0, 0)),
            pl.BlockSpec((1, 3, D), lambda b, t: (BATCH, 0, 0)),
            pl.BlockSpec((1, D), const2),
            pl.BlockSpec((D, IN_COLS), const2),
            pl.BlockSpec((1, HD), const2),
            pl.BlockSpec((1, HD), const2),
            table, table, table, table,
        ],
        out_specs=[
            pl.BlockSpec((1, ROWS, KV_COLS), lambda b, t: (b, t, 0)),
            pl.BlockSpec((1, ROWS, D), xrow),
            pl.BlockSpec((1, ROWS, D), xrow),
        ],
        out_shape=[
            jax.ShapeDtypeStruct((BATCH, KEYS, KV_COLS), BF16),
            jax.ShapeDtypeStruct((BATCH, SEQ, D), BF16),
            jax.ShapeDtypeStruct((BATCH, SEQ, D), BF16),
        ],
        compiler_params=pltpu.CompilerParams(
            dimension_semantics=("arbitrary", "arbitrary"), vmem_limit_bytes=VMEM_LIMIT),
        name="inproj",
    )(x, ctx, mod, mod, norm_g.reshape(1, D), w_in.astype(BF16), qn_g.reshape(1, HD), kn_g.reshape(1, HD),
      _COS_A, _SIN_A, _COS_B, _SIN_B)


def _attn_kernel(lq1_ref, lk1_ref, lq2_ref, lk2_ref, subln_ref, q_ref, kv_ref, gate_ref, o_ref, *, lam_init):
    nq = q_ref.shape[1]

    def softmax_pv(q2, kcol, vcol):
        m_rows = q2.shape[0]

        def body(c, carry):
            m, l, acc = carry
            r0 = pl.multiple_of(c * TK, TK)
            k = kv_ref[0, pl.ds(r0, TK), kcol:kcol + HD]
            v = kv_ref[0, pl.ds(r0, TK), vcol:vcol + HD]
            s = lax.dot_general(q2, k, (((1,), (1,)), ((), ())), preferred_element_type=F32)
            m_new = jnp.maximum(m, jnp.max(s, axis=-1, keepdims=True))
            alpha = jnp.exp(m - m_new)
            p = jnp.exp(s - m_new)
            l = alpha * l + jnp.sum(p, axis=-1, keepdims=True)
            acc = alpha * acc + jnp.dot(p.astype(BF16), v, preferred_element_type=F32)
            return m_new, l, acc

        init = (jnp.full((m_rows, 1), -jnp.inf, F32), jnp.zeros((m_rows, 1), F32), jnp.zeros((m_rows, HD), F32))
        _, l, acc = lax.fori_loop(0, KEYS // TK, body, init)
        return acc * (1.0 / l)

    group = A_HEADS // A_KV_HEADS
    for kh in range(A_KV_HEADS):
        q2 = jnp.concatenate([q_ref[0, :, (kh * group + g) * HD:(kh * group + g + 1) * HD] for g in range(group)], axis=0)
        o = softmax_pv(q2, KA0 + kh * HD, VA0 + kh * HD)
        for g in range(group):
            h = kh * group + g
            o_ref[0, :, h * HD:(h + 1) * HD] = (o[g * nq:(g + 1) * nq] * gate_ref[0, :, h * HD:(h + 1) * HD]).astype(BF16)

    lam = (jnp.exp(jnp.sum(lq1_ref[...] * lk1_ref[...], keepdims=True))
           - jnp.exp(jnp.sum(lq2_ref[...] * lk2_ref[...], keepdims=True)) + lam_init)
    lane = lax.broadcasted_iota(jnp.int32, (nq, HD), 1)
    for h in range(B_HEADS):
        c0 = (A_HEADS + h) * HD
        qh = q_ref[0, :, c0:c0 + HD]
        zero = jnp.zeros_like(qh)
        q12 = jnp.concatenate([jnp.where(lane < B_HEAD_DIM, qh, zero), jnp.where(lane >= B_HEAD_DIM, qh, zero)], axis=0)
        o12 = softmax_pv(q12, KB0 + h * HD, VB0 + h * HD)
        o = o12[:nq] - lam * o12[nq:]
        o = _rms_rows(o) * subln_ref[...] * (1.0 - lam_init)
        o_ref[0, :, c0:c0 + HD] = (o * gate_ref[0, :, c0:c0 + HD]).astype(BF16)


def _attention(q, kv, gate, lq1, lk1, lq2, lk2, subln_g, lam_init):
    const2 = lambda b, i: (0, 0)
    lamspec = pl.BlockSpec((1, B_HEAD_DIM), const2)
    qspec = pl.BlockSpec((1, TQ, D), lambda b, i: (b, i, 0))
    return pl.pallas_call(
        functools.partial(_attn_kernel, lam_init=lam_init),
        grid=(BATCH, SEQ // TQ),
        in_specs=[lamspec, lamspec, lamspec, lamspec,
                  pl.BlockSpec((1, HD), const2),
                  qspec,
                  pl.BlockSpec((1, KEYS, KV_COLS), lambda b, i: (b, 0, 0)),
                  qspec],
        out_specs=qspec,
        out_shape=jax.ShapeDtypeStruct((BATCH, SEQ, D), BF16),
        compiler_params=pltpu.CompilerParams(
            dimension_semantics=("arbitrary", "arbitrary"), vmem_limit_bytes=VMEM_LIMIT),
        name="attn",
    )(lq1.reshape(1, -1), lk1.reshape(1, -1), lq2.reshape(1, -1), lk2.reshape(1, -1),
      subln_g.reshape(1, HD), q, kv, gate)


def _channel_dft():
    c = np.arange(F_GROUP_DIM)
    ang = 2.0 * np.pi * ((c[:, None] * c[None, :]) % F_GROUP_DIM) / F_GROUP_DIM
    return (np.concatenate([np.cos(ang), np.sin(ang)], axis=1) / 16.0).astype(np.float32)


_CHANNEL_DFT = _channel_dft()


def _mid_kernel(x_ref, og_ref, mod0_ref, mod1_ref, g_ref, wo_ref, wf_ref, cs_ref, x1_ref, ab_ref, sg_ref):
    x1 = x_ref[0] + mod0_ref[0, 2:3, :] * jnp.dot(og_ref[0], wo_ref[...], preferred_element_type=F32)
    x1_ref[0] = x1
    mod = mod1_ref[0]
    hb = (_rms_rows(x1) * (g_ref[...] * (1.0 + mod[1:2, :])) + mod[0:1, :]).astype(BF16)
    u = jnp.dot(hb, wf_ref[:, 0:D], preferred_element_type=F32).astype(BF16)
    for gi in range(F_GROUPS):
        c0 = gi * F_GROUP_DIM
        ab = jnp.dot(u[:, c0:c0 + F_GROUP_DIM], cs_ref[...], preferred_element_type=F32)
        ab_ref[0, 0, :, c0:c0 + F_GROUP_DIM] = ab[:, 0:F_GROUP_DIM].astype(BF16)
        ab_ref[0, 1, :, c0:c0 + F_GROUP_DIM] = ab[:, F_GROUP_DIM:].astype(BF16)
    sg_ref[0] = _silu(jnp.dot(hb, wf_ref[:, D:2 * D], preferred_element_type=F32)).astype(BF16)


def _mid(x, og, mod0, mod1, norm_g, w_out, w_f):
    const2 = lambda b, i: (0, 0)
    row = lambda b, i: (b, i, 0)
    modspec = pl.BlockSpec((1, 3, D), lambda b, i: (b, 0, 0))
    return pl.pallas_call(
        _mid_kernel,
        grid=(BATCH, SEQ // ROWS),
        in_specs=[pl.BlockSpec((1, ROWS, D), row), pl.BlockSpec((1, ROWS, D), row), modspec, modspec,
                  pl.BlockSpec((1, D), const2), pl.BlockSpec((D, D), const2), pl.BlockSpec((D, 2 * D), const2),
                  pl.BlockSpec((F_GROUP_DIM, 2 * F_GROUP_DIM), const2)],
        out_specs=[pl.BlockSpec((1, ROWS, D), row),
                   pl.BlockSpec((1, 2, ROWS, D), lambda b, i: (b, 0, i, 0)),
                   pl.BlockSpec((1, ROWS, D), row)],
        out_shape=[jax.ShapeDtypeStruct((BATCH, SEQ, D), F32),
                   jax.ShapeDtypeStruct((BATCH, 2, SEQ, D), BF16),
                   jax.ShapeDtypeStruct((BATCH, SEQ, D), BF16)],
        compiler_params=pltpu.CompilerParams(
            dimension_semantics=("arbitrary", "arbitrary"), vmem_limit_bytes=VMEM_LIMIT),
        name="mid",
    )(x, og, mod0, mod1, norm_g.reshape(1, D), w_out.astype(BF16), w_f.astype(BF16),
      jnp.asarray(_CHANNEL_DFT).astype(BF16))


def _position_dft():
    k = lax.broadcasted_iota(jnp.int32, (SEQ, SEQ), 0)
    n = lax.broadcasted_iota(jnp.int32, (SEQ, SEQ), 1)
    ang = ((k * n) & (SEQ - 1)).astype(F32) * (2.0 * math.pi / SEQ)
    return jnp.concatenate([jnp.cos(ang) * (1.0 / 64.0), jnp.sin(ang) * (-1.0 / 64.0)], axis=1).astype(BF16)


def _fourier_kernel(dft_ref, ab_ref, sg_ref, x1_ref, mod_ref, wo_ref, fg_ref, o_ref):
    f = jnp.dot(dft_ref[...], ab_ref[0], preferred_element_type=F32)
    fg = (f * sg_ref[0].astype(F32)).astype(BF16)
    x2 = x1_ref[0] + mod_ref[0, 2:3, :] * jnp.dot(fg, wo_ref[...], preferred_element_type=F32)
    o_ref[0] = _rms_rows(x2) * fg_ref[...]


def _fourier(dft, ab, sg, x1, mod1, w_out, final_g):
    const2 = lambda b, i: (0, 0)
    row = lambda b, i: (b, i, 0)
    return pl.pallas_call(
        _fourier_kernel,
        grid=(BATCH, SEQ // ROWS),
        in_specs=[pl.BlockSpec((ROWS, 2 * SEQ), lambda b, i: (i, 0)),
                  pl.BlockSpec((1, 2 * SEQ, D), lambda b, i: (b, 0, 0), pipeline_mode=pl.Buffered(1)),
                  pl.BlockSpec((1, ROWS, D), row), pl.BlockSpec((1, ROWS, D), row),
                  pl.BlockSpec((1, 3, D), lambda b, i: (b, 0, 0)),
                  pl.BlockSpec((D, D), const2), pl.BlockSpec((1, D), const2)],
        out_specs=pl.BlockSpec((1, ROWS, D), row),
        out_shape=jax.ShapeDtypeStruct((BATCH, SEQ, D), F32),
        compiler_params=pltpu.CompilerParams(
            dimension_semantics=("arbitrary", "arbitrary"), vmem_limit_bytes=VMEM_LIMIT),
        name="fourier",
    )(dft, ab, sg, x1, mod1, w_out.astype(BF16), final_g.reshape(1, D))


def kernel(x, c, ctx, c_ctx, ada_w, ada_b, norm_g, attn_in_w, attn_qn_g, attn_kn_g, lam_q1, lam_k1, lam_q2, lam_k2,
           attn_subln_g, attn_out_w, fourier_in_w, fourier_out_w, final_g):
    assert x.shape == (BATCH, SEQ, D) and ctx.shape == (BATCH, CTX, D)
    assert ada_w.shape[0] == 2 and attn_in_w.shape == (1, D, IN_COLS)
    cond = jnp.concatenate([c, c_ctx[None, :], jnp.zeros((16 - BATCH - 1, D), F32)], axis=0)
    mods = _ada(cond, ada_w, ada_b)
    mod0 = mods[0].reshape(16, 3, D)
    mod1 = mods[1].reshape(16, 3, D)
    lam_init = 0.8 - 0.6 * math.exp(-0.3 * 0)

    kv, q, gate = _inproj(x, ctx, mod0, norm_g[0], attn_in_w[0], attn_qn_g[0], attn_kn_g[0])
    og = _attention(q, kv, gate, lam_q1[0], lam_k1[0], lam_q2[0], lam_k2[0], attn_subln_g[0], lam_init)
    x1, ab, sg = _mid(x, og, mod0, mod1, norm_g[1], attn_out_w[0], fourier_in_w[0])
    return _fourier(_position_dft(), ab.reshape(BATCH, 2 * SEQ, D), sg, x1, mod1, fourier_out_w[0], final_g)
```

```python
import functools
import math

import numpy as np
import jax
import jax.numpy as jnp
from jax import lax
from jax.experimental import pallas as pl
from jax.experimental.pallas import tpu as pltpu

F32 = jnp.float32
BF16 = jnp.bfloat16

D = 1024
BATCH = 8
SEQ = 4096
CTX = 256
KEYS = CTX + SEQ
GRID_W = 64
ROPE_THETA = 10000.0
EPS = 1e-6
LOG2E = math.log2(math.e)

HD = 128
A_HEADS = 4
A_KV_HEADS = 2
B_HEADS = 4
B_HEAD_DIM = 64
KA0, VA0, KB0, VB0 = 0, 256, 512, 1024
K_COLS = 768
V_ROWS = 768
QA0, QB0, G0 = 1536, 2048, 2560
IN_COLS = 3584
F_GROUPS = 4
F_GROUP_DIM = 256

ROWS = 256
TQ = 256
TK = 256
VMEM_LIMIT = 56 * 1024 * 1024


def _silu(v):
    return v * (1.0 / (1.0 + jnp.exp(-v)))


def _rms_rows(v):
    return v * lax.rsqrt(jnp.mean(v * v, axis=-1, keepdims=True) + EPS)


def _ada_kernel(c_ref, w_ref, b_ref, o_ref):
    s = _silu(c_ref[...]).astype(BF16)
    o_ref[0] = jnp.dot(s, w_ref[0].astype(BF16), preferred_element_type=F32) + b_ref[0]


def _ada(cond, ada_w, ada_b):
    depth, _, n3 = ada_w.shape
    tn = 768
    return pl.pallas_call(
        _ada_kernel,
        grid=(depth, n3 // tn),
        in_specs=[
            pl.BlockSpec((16, D), lambda l, j: (0, 0)),
            pl.BlockSpec((1, D, tn), lambda l, j: (l, 0, j)),
            pl.BlockSpec((1, 1, tn), lambda l, j: (l, 0, j)),
        ],
        out_specs=pl.BlockSpec((1, 16, tn), lambda l, j: (l, 0, j)),
        out_shape=jax.ShapeDtypeStruct((depth, 16, n3), F32),
        compiler_params=pltpu.CompilerParams(
            dimension_semantics=("arbitrary", "arbitrary"), vmem_limit_bytes=VMEM_LIMIT),
        name="ada",
    )(cond, ada_w, ada_b.reshape(depth, 1, n3))


def _rope_table(dim):
    q4 = dim // 4
    pos = np.arange(SEQ)
    axis_pos = np.stack([pos // GRID_W, pos % GRID_W], axis=1).astype(np.float64)
    inv = ROPE_THETA ** (-np.arange(q4, dtype=np.float64) / q4)
    ang = axis_pos[:, :, None, None] * inv[None, None, None, :]
    ang = np.broadcast_to(ang, (SEQ, 2, 2, q4))
    sign = np.array([-1.0, 1.0]).reshape(1, 1, 2, 1)
    cos = np.cos(ang).reshape(SEQ, dim)
    sin = (np.sin(ang) * sign).reshape(SEQ, dim)
    return cos.astype(np.float32), sin.astype(np.float32)


_COS_A, _SIN_A = _rope_table(HD)
_cb, _sb = _rope_table(B_HEAD_DIM)
_COS_B, _SIN_B = np.tile(_cb, (1, 2)), np.tile(_sb, (1, 2))


def _rope(y, cos, sin, half):
    lane = lax.broadcasted_iota(jnp.int32, y.shape, 1)
    first = (lane & (2 * half - 1)) < half
    partner = jnp.where(first, pltpu.roll(y, HD - half, 1), pltpu.roll(y, half, 1))
    return y * cos + partner * sin


def _inproj_kernel(x_ref, ctx_ref, modx_ref, modc_ref, g_ref, w_ref, wvt_ref, qn_ref, kn_ref,
                   cosa_ref, sina_ref, cosb_ref, sinb_ref, k_ref, vt_ref, q_ref, gate_ref):
    t = pl.program_id(1)

    def hidden(src, mod):
        a = g_ref[...] * (1.0 + mod[1:2, :])
        return (_rms_rows(src) * a + mod[0:1, :]).astype(BF16)

    def proj(hb, c0, width):
        return jnp.dot(hb, w_ref[:, c0:c0 + width], preferred_element_type=F32)

    def keys_values(hb, rope):
        ka = proj(hb, KA0, A_KV_HEADS * HD)
        for h in range(A_KV_HEADS):
            y = _rms_rows(ka[:, h * HD:(h + 1) * HD]) * kn_ref[...]
            if rope:
                y = _rope(y, cosa_ref[...], sina_ref[...], HD // 4)
            k_ref[0, :, h * HD:(h + 1) * HD] = y.astype(BF16)
        kb = proj(hb, KB0, B_HEADS * HD)
        for h in range(B_HEADS):
            y = kb[:, h * HD:(h + 1) * HD]
            if rope:
                y = _rope(y, cosb_ref[...], sinb_ref[...], B_HEAD_DIM // 4)
            k_ref[0, :, (A_KV_HEADS + h) * HD:(A_KV_HEADS + h + 1) * HD] = y.astype(BF16)
        vt = lax.dot_general(wvt_ref[...], hb, (((1,), (1,)), ((), ())), preferred_element_type=F32)
        vt_ref[0, 0] = vt.astype(BF16)

    @pl.when(t == 0)
    def _():
        keys_values(hidden(ctx_ref[0], modc_ref[0]), rope=False)

    @pl.when(t > 0)
    def _():
        hb = hidden(x_ref[0], modx_ref[0])
        keys_values(hb, rope=True)
        qa = proj(hb, QA0, A_HEADS * HD)
        qgain = qn_ref[...] * (HD ** -0.5 * LOG2E)
        for h in range(A_HEADS):
            y = _rms_rows(qa[:, h * HD:(h + 1) * HD]) * qgain
            y = _rope(y, cosa_ref[...], sina_ref[...], HD // 4)
            q_ref[0, :, h * HD:(h + 1) * HD] = y.astype(BF16)
        qb = proj(hb, QB0, B_HEADS * HD)
        for h in range(B_HEADS):
            y = _rope(qb[:, h * HD:(h + 1) * HD], cosb_ref[...], sinb_ref[...], B_HEAD_DIM // 4)
            q_ref[0, :, (A_HEADS + h) * HD:(A_HEADS + h + 1) * HD] = (y * (B_HEAD_DIM ** -0.5 * LOG2E)).astype(BF16)
        gate_ref[0] = _silu(proj(hb, G0, D)).astype(BF16)


def _inproj(x, ctx, mod, norm_g, w_in, qn_g, kn_g):
    nt = KEYS // ROWS
    xrow = lambda b, t: (b, jnp.maximum(t - 1, 0), 0)
    trow = lambda b, t: (jnp.maximum(t - 1, 0), 0)
    const2 = lambda b, t: (0, 0)
    table = pl.BlockSpec((ROWS, HD), trow)
    wb = w_in.astype(BF16)
    wvt = jnp.concatenate([wb[:, VA0:VA0 + A_KV_HEADS * HD], wb[:, VB0:VB0 + B_HEADS * HD]], axis=1).T
    return pl.pallas_call(
        _inproj_kernel,
        grid=(BATCH, nt),
        in_specs=[
            pl.BlockSpec((1, ROWS, D), xrow),
            pl.BlockSpec((1, CTX, D), lambda b, t: (b, 0, 0)),
            pl.BlockSpec((1, 3, D), lambda b, t: (b, 0, 0)),
            pl.BlockSpec((1, 3, D), lambda b, t: (BATCH, 0, 0)),
            pl.BlockSpec((1, D), const2),
            pl.BlockSpec((D, IN_COLS), const2),
            pl.BlockSpec((V_ROWS, D), const2),
            pl.BlockSpec((1, HD), const2),
            pl.BlockSpec((1, HD), const2),
            table, table, table, table,
        ],
        out_specs=[
            pl.BlockSpec((1, ROWS, K_COLS), lambda b, t: (b, t, 0)),
            pl.BlockSpec((1, 1, V_ROWS, ROWS), lambda b, t: (b, t, 0, 0)),
            pl.BlockSpec((1, ROWS, D), xrow),
            pl.BlockSpec((1, ROWS, D), xrow),
        ],
        out_shape=[
            jax.ShapeDtypeStruct((BATCH, KEYS, K_COLS), BF16),
            jax.ShapeDtypeStruct((BATCH, nt, V_ROWS, ROWS), BF16),
            jax.ShapeDtypeStruct((BATCH, SEQ, D), BF16),
            jax.ShapeDtypeStruct((BATCH, SEQ, D), BF16),
        ],
        compiler_params=pltpu.CompilerParams(
            dimension_semantics=("arbitrary", "arbitrary"), vmem_limit_bytes=VMEM_LIMIT),
        name="inproj",
    )(x, ctx, mod, mod, norm_g.reshape(1, D), wb, wvt, qn_g.reshape(1, HD), kn_g.reshape(1, HD),
      _COS_A, _SIN_A, _COS_B, _SIN_B)


N_CHAINS = A_KV_HEADS + B_HEADS
ACC_ROWS = HD + 16


def _attn_kernel(lq1_ref, lk1_ref, lq2_ref, lk2_ref, subln_ref, q_ref, k_ref, vt_ref, gate_ref, o_ref,
                 acc_ref, m_ref, *, lam_init):
    nq = q_ref.shape[1]
    group = A_HEADS // A_KV_HEADS
    lane = lax.broadcasted_iota(jnp.int32, (nq, HD), 1)

    q2 = []
    for kh in range(A_KV_HEADS):
        q2.append(jnp.concatenate([q_ref[0, :, (kh * group + g) * HD:(kh * group + g + 1) * HD]
                                   for g in range(group)], axis=0))
    for h in range(B_HEADS):
        qh = q_ref[0, :, (A_HEADS + h) * HD:(A_HEADS + h + 1) * HD]
        zero = jnp.zeros_like(qh)
        q2.append(jnp.concatenate([jnp.where(lane < B_HEAD_DIM, qh, zero),
                                   jnp.where(lane >= B_HEAD_DIM, qh, zero)], axis=0))

    m_ref[...] = jnp.full(m_ref.shape, -jnp.inf, F32)
    acc_ref[...] = jnp.zeros(acc_ref.shape, F32)
    ones = jnp.concatenate([jnp.ones((1, TK), BF16), jnp.zeros((ACC_ROWS - HD - 1, TK), BF16)], axis=0)

    def body(c, carry):
        r0 = pl.multiple_of(c * TK, TK)

        def scores(ci):
            k = k_ref[0, pl.ds(r0, TK), ci * HD:(ci + 1) * HD]
            return lax.dot_general(k, q2[ci], (((1,), (1,)), ((), ())), preferred_element_type=F32)

        st_next = scores(0)
        for ci in range(N_CHAINS):
            st = st_next
            if ci + 1 < N_CHAINS:
                st_next = scores(ci + 1)
            vt = jnp.concatenate([vt_ref[0, c, ci * HD:(ci + 1) * HD, :], ones], axis=0)
            m_old = m_ref[ci]
            m_new = jnp.maximum(m_old, jnp.max(st, axis=0, keepdims=True))
            alpha = jnp.exp2(m_old - m_new)
            pt = jnp.exp2(st - m_new).astype(BF16)
            acc_ref[ci] = alpha * acc_ref[ci] + jnp.dot(vt, pt, preferred_element_type=F32)
            m_ref[ci] = m_new
        return carry

    lax.fori_loop(0, KEYS // TK, body, 0)

    def normalized(ci):
        acc = acc_ref[ci]
        return acc[0:HD] * (1.0 / acc[HD:HD + 1])

    for kh in range(A_KV_HEADS):
        ot = normalized(kh)
        for g in range(group):
            h = kh * group + g
            o = ot[:, g * nq:(g + 1) * nq].T
            o_ref[0, :, h * HD:(h + 1) * HD] = (o * gate_ref[0, :, h * HD:(h + 1) * HD]).astype(BF16)

    lam = (jnp.exp(jnp.sum(lq1_ref[...] * lk1_ref[...], keepdims=True))
           - jnp.exp(jnp.sum(lq2_ref[...] * lk2_ref[...], keepdims=True)) + lam_init)
    for h in range(B_HEADS):
        c0 = (A_HEADS + h) * HD
        ot = normalized(A_KV_HEADS + h)
        od = ot[:, 0:nq] - lam * ot[:, nq:2 * nq]
        od = od * lax.rsqrt(jnp.mean(od * od, axis=0, keepdims=True) + EPS)
        o = od.T * subln_ref[...] * (1.0 - lam_init)
        o_ref[0, :, c0:c0 + HD] = (o * gate_ref[0, :, c0:c0 + HD]).astype(BF16)


def _attention(q, k, vt, gate, lq1, lk1, lq2, lk2, subln_g, lam_init):
    const2 = lambda b, i: (0, 0)
    lamspec = pl.BlockSpec((1, B_HEAD_DIM), const2)
    qspec = pl.BlockSpec((1, TQ, D), lambda b, i: (b, i, 0))
    return pl.pallas_call(
        functools.partial(_attn_kernel, lam_init=lam_init),
        grid=(BATCH, SEQ // TQ),
        in_specs=[lamspec, lamspec, lamspec, lamspec,
                  pl.BlockSpec((1, HD), const2),
                  qspec,
                  pl.BlockSpec((1, KEYS, K_COLS), lambda b, i: (b, 0, 0)),
                  pl.BlockSpec((1, KEYS // TK, V_ROWS, TK), lambda b, i: (b, 0, 0, 0)),
                  qspec],
        out_specs=qspec,
        out_shape=jax.ShapeDtypeStruct((BATCH, SEQ, D), BF16),
        scratch_shapes=[pltpu.VMEM((N_CHAINS, ACC_ROWS, 2 * TQ), F32),
                        pltpu.VMEM((N_CHAINS, 1, 2 * TQ), F32)],
        compiler_params=pltpu.CompilerParams(
            dimension_semantics=("arbitrary", "arbitrary"), vmem_limit_bytes=VMEM_LIMIT),
        name="attn",
    )(lq1.reshape(1, -1), lk1.reshape(1, -1), lq2.reshape(1, -1), lk2.reshape(1, -1),
      subln_g.reshape(1, HD), q, k, vt, gate)


def _channel_dft():
    c = np.arange(F_GROUP_DIM)
    ang = 2.0 * np.pi * ((c[:, None] * c[None, :]) % F_GROUP_DIM) / F_GROUP_DIM
    return (np.concatenate([np.cos(ang), np.sin(ang)], axis=1) / 16.0).astype(np.float32)


_CHANNEL_DFT = _channel_dft()


def _mid_kernel(x_ref, og_ref, mod0_ref, mod1_ref, g_ref, wo_ref, wf_ref, cs_ref, x1_ref, ab_ref, sg_ref):
    x1 = x_ref[0] + mod0_ref[0, 2:3, :] * jnp.dot(og_ref[0], wo_ref[...], preferred_element_type=F32)
    x1_ref[0] = x1
    mod = mod1_ref[0]
    hb = (_rms_rows(x1) * (g_ref[...] * (1.0 + mod[1:2, :])) + mod[0:1, :]).astype(BF16)
    u = jnp.dot(hb, wf_ref[:, 0:D], preferred_element_type=F32).astype(BF16)
    for gi in range(F_GROUPS):
        c0 = gi * F_GROUP_DIM
        ab = jnp.dot(u[:, c0:c0 + F_GROUP_DIM], cs_ref[...], preferred_element_type=F32)
        ab_ref[0, 0, :, c0:c0 + F_GROUP_DIM] = ab[:, 0:F_GROUP_DIM].astype(BF16)
        ab_ref[0, 1, :, c0:c0 + F_GROUP_DIM] = ab[:, F_GROUP_DIM:].astype(BF16)
    sg_ref[0] = _silu(jnp.dot(hb, wf_ref[:, D:2 * D], preferred_element_type=F32)).astype(BF16)


def _mid(x, og, mod0, mod1, norm_g, w_out, w_f):
    const2 = lambda b, i: (0, 0)
    row = lambda b, i: (b, i, 0)
    modspec = pl.BlockSpec((1, 3, D), lambda b, i: (b, 0, 0))
    return pl.pallas_call(
        _mid_kernel,
        grid=(BATCH, SEQ // ROWS),
        in_specs=[pl.BlockSpec((1, ROWS, D), row), pl.BlockSpec((1, ROWS, D), row), modspec, modspec,
                  pl.BlockSpec((1, D), const2), pl.BlockSpec((D, D), const2), pl.BlockSpec((D, 2 * D), const2),
                  pl.BlockSpec((F_GROUP_DIM, 2 * F_GROUP_DIM), const2)],
        out_specs=[pl.BlockSpec((1, ROWS, D), row),
                   pl.BlockSpec((1, 2, ROWS, D), lambda b, i: (b, 0, i, 0)),
                   pl.BlockSpec((1, ROWS, D), row)],
        out_shape=[jax.ShapeDtypeStruct((BATCH, SEQ, D), F32),
                   jax.ShapeDtypeStruct((BATCH, 2, SEQ, D), BF16),
                   jax.ShapeDtypeStruct((BATCH, SEQ, D), BF16)],
        compiler_params=pltpu.CompilerParams(
            dimension_semantics=("arbitrary", "arbitrary"), vmem_limit_bytes=VMEM_LIMIT),
        name="mid",
    )(x, og, mod0, mod1, norm_g.reshape(1, D), w_out.astype(BF16), w_f.astype(BF16),
      jnp.asarray(_CHANNEL_DFT).astype(BF16))


def _position_dft():
    k = lax.broadcasted_iota(jnp.int32, (SEQ, SEQ), 0)
    n = lax.broadcasted_iota(jnp.int32, (SEQ, SEQ), 1)
    ang = ((k * n) & (SEQ - 1)).astype(F32) * (2.0 * math.pi / SEQ)
    return jnp.concatenate([jnp.cos(ang) * (1.0 / 64.0), jnp.sin(ang) * (-1.0 / 64.0)], axis=1).astype(BF16)


def _fourier_kernel(dft_ref, ab_ref, sg_ref, x1_ref, mod_ref, wo_ref, fg_ref, o_ref):
    f = jnp.dot(dft_ref[...], ab_ref[0], preferred_element_type=F32)
    fg = (f * sg_ref[0].astype(F32)).astype(BF16)
    x2 = x1_ref[0] + mod_ref[0, 2:3, :] * jnp.dot(fg, wo_ref[...], preferred_element_type=F32)
    o_ref[0] = _rms_rows(x2) * fg_ref[...]


def _fourier(dft, ab, sg, x1, mod1, w_out, final_g):
    const2 = lambda b, i: (0, 0)
    row = lambda b, i: (b, i, 0)
    return pl.pallas_call(
        _fourier_kernel,
        grid=(BATCH, SEQ // ROWS),
        in_specs=[pl.BlockSpec((ROWS, 2 * SEQ), lambda b, i: (i, 0)),
                  pl.BlockSpec((1, 2 * SEQ, D), lambda b, i: (b, 0, 0), pipeline_mode=pl.Buffered(1)),
                  pl.BlockSpec((1, ROWS, D), row), pl.BlockSpec((1, ROWS, D), row),
                  pl.BlockSpec((1, 3, D), lambda b, i: (b, 0, 0)),
                  pl.BlockSpec((D, D), const2), pl.BlockSpec((1, D), const2)],
        out_specs=pl.BlockSpec((1, ROWS, D), row),
        out_shape=jax.ShapeDtypeStruct((BATCH, SEQ, D), F32),
        compiler_params=pltpu.CompilerParams(
            dimension_semantics=("arbitrary", "arbitrary"), vmem_limit_bytes=VMEM_LIMIT),
        name="fourier",
    )(dft, ab, sg, x1, mod1, w_out.astype(BF16), final_g.reshape(1, D))


def kernel(x, c, ctx, c_ctx, ada_w, ada_b, norm_g, attn_in_w, attn_qn_g, attn_kn_g, lam_q1, lam_k1, lam_q2, lam_k2,
           attn_subln_g, attn_out_w, fourier_in_w, fourier_out_w, final_g):
    assert x.shape == (BATCH, SEQ, D) and ctx.shape == (BATCH, CTX, D)
    assert ada_w.shape[0] == 2 and attn_in_w.shape == (1, D, IN_COLS)
    cond = jnp.concatenate([c, c_ctx[None, :], jnp.zeros((16 - BATCH - 1, D), F32)], axis=0)
    mods = _ada(cond, ada_w, ada_b)
    mod0 = mods[0].reshape(16, 3, D)
    mod1 = mods[1].reshape(16, 3, D)
    lam_init = 0.8 - 0.6 * math.exp(-0.3 * 0)

    k, vt, q, gate = _inproj(x, ctx, mod0, norm_g[0], attn_in_w[0], attn_qn_g[0], attn_kn_g[0])
    og = _attention(q, k, vt, gate, lam_q1[0], lam_k1[0], lam_q2[0], lam_k2[0], attn_subln_g[0], lam_init)
    x1, ab, sg = _mid(x, og, mod0, mod1, norm_g[1], attn_out_w[0], fourier_in_w[0])
    return _fourier(_position_dft(), ab.reshape(BATCH, 2 * SEQ, D), sg, x1, mod1, fourier_out_w[0], final_g)
```

```python
import functools
import math

import numpy as np
import jax
import jax.numpy as jnp
from jax import lax
from jax.experimental import pallas as pl
from jax.experimental.pallas import tpu as pltpu

F32 = jnp.float32
BF16 = jnp.bfloat16

D = 1024
BATCH = 8
SEQ = 4096
CTX = 256
KEYS = CTX + SEQ
GRID_W = 64
ROPE_THETA = 10000.0
EPS = 1e-6
LOG2E = math.log2(math.e)

HD = 128
A_HEADS = 4
A_KV_HEADS = 2
B_HEADS = 4
B_HEAD_DIM = 64
KA0, VA0, KB0, VB0 = 0, 256, 512, 1024
K_COLS = 768
V_ROWS = 768
QA0, QB0, G0 = 1536, 2048, 2560
IN_COLS = 3584
F_GROUPS = 4
F_GROUP_DIM = 256

ROWS = 256
TQ = 256
TK = 256
VMEM_LIMIT = 56 * 1024 * 1024


def _silu(v):
    return v * (1.0 / (1.0 + jnp.exp(-v)))


def _rms_rows(v):
    return v * lax.rsqrt(jnp.mean(v * v, axis=-1, keepdims=True) + EPS)


def _ada_kernel(c_ref, w_ref, b_ref, o_ref):
    s = _silu(c_ref[...]).astype(BF16)
    o_ref[0] = jnp.dot(s, w_ref[0].astype(BF16), preferred_element_type=F32) + b_ref[0]


def _ada(cond, ada_w, ada_b):
    depth, _, n3 = ada_w.shape
    tn = 768
    return pl.pallas_call(
        _ada_kernel,
        grid=(depth, n3 // tn),
        in_specs=[
            pl.BlockSpec((16, D), lambda l, j: (0, 0)),
            pl.BlockSpec((1, D, tn), lambda l, j: (l, 0, j)),
            pl.BlockSpec((1, 1, tn), lambda l, j: (l, 0, j)),
        ],
        out_specs=pl.BlockSpec((1, 16, tn), lambda l, j: (l, 0, j)),
        out_shape=jax.ShapeDtypeStruct((depth, 16, n3), F32),
        compiler_params=pltpu.CompilerParams(
            dimension_semantics=("arbitrary", "arbitrary"), vmem_limit_bytes=VMEM_LIMIT),
        name="ada",
    )(cond, ada_w, ada_b.reshape(depth, 1, n3))


def _rope_table(dim):
    q4 = dim // 4
    pos = np.arange(SEQ)
    axis_pos = np.stack([pos // GRID_W, pos % GRID_W], axis=1).astype(np.float64)
    inv = ROPE_THETA ** (-np.arange(q4, dtype=np.float64) / q4)
    ang = axis_pos[:, :, None, None] * inv[None, None, None, :]
    ang = np.broadcast_to(ang, (SEQ, 2, 2, q4))
    sign = np.array([-1.0, 1.0]).reshape(1, 1, 2, 1)
    cos = np.cos(ang).reshape(SEQ, dim)
    sin = (np.sin(ang) * sign).reshape(SEQ, dim)
    return cos.astype(np.float32), sin.astype(np.float32)


_COS_A, _SIN_A = _rope_table(HD)
_cb, _sb = _rope_table(B_HEAD_DIM)
_COS_B, _SIN_B = np.tile(_cb, (1, 2)), np.tile(_sb, (1, 2))


def _rope(y, cos, sin, half):
    lane = lax.broadcasted_iota(jnp.int32, y.shape, 1)
    first = (lane & (2 * half - 1)) < half
    partner = jnp.where(first, pltpu.roll(y, HD - half, 1), pltpu.roll(y, half, 1))
    return y * cos + partner * sin


def _inproj_kernel(x_ref, ctx_ref, modx_ref, modc_ref, g_ref, w_ref, wvt_ref, qn_ref, kn_ref,
                   cosa_ref, sina_ref, cosb_ref, sinb_ref, k_ref, vt_ref, q_ref, gate_ref):
    t = pl.program_id(1)

    def hidden(src, mod):
        a = g_ref[...] * (1.0 + mod[1:2, :])
        return (_rms_rows(src) * a + mod[0:1, :]).astype(BF16)

    def proj(hb, c0, width):
        return jnp.dot(hb, w_ref[:, c0:c0 + width], preferred_element_type=F32)

    def keys_values(hb, rope):
        ka = proj(hb, KA0, A_KV_HEADS * HD)
        for h in range(A_KV_HEADS):
            y = _rms_rows(ka[:, h * HD:(h + 1) * HD]) * kn_ref[...]
            if rope:
                y = _rope(y, cosa_ref[...], sina_ref[...], HD // 4)
            k_ref[0, :, h * HD:(h + 1) * HD] = y.astype(BF16)
        kb = proj(hb, KB0, B_HEADS * HD)
        for h in range(B_HEADS):
            y = kb[:, h * HD:(h + 1) * HD]
            if rope:
                y = _rope(y, cosb_ref[...], sinb_ref[...], B_HEAD_DIM // 4)
            k_ref[0, :, (A_KV_HEADS + h) * HD:(A_KV_HEADS + h + 1) * HD] = y.astype(BF16)
        vt = lax.dot_general(wvt_ref[...], hb, (((1,), (1,)), ((), ())), preferred_element_type=F32)
        vt_ref[0, 0] = vt.astype(BF16)

    @pl.when(t == 0)
    def _():
        keys_values(hidden(ctx_ref[0], modc_ref[0]), rope=False)

    @pl.when(t > 0)
    def _():
        hb = hidden(x_ref[0], modx_ref[0])
        keys_values(hb, rope=True)
        qa = proj(hb, QA0, A_HEADS * HD)
        qgain = qn_ref[...] * (HD ** -0.5 * LOG2E)
        for h in range(A_HEADS):
            y = _rms_rows(qa[:, h * HD:(h + 1) * HD]) * qgain
            y = _rope(y, cosa_ref[...], sina_ref[...], HD // 4)
            q_ref[0, :, h * HD:(h + 1) * HD] = y.astype(BF16)
        qb = proj(hb, QB0, B_HEADS * HD)
        for h in range(B_HEADS):
            y = _rope(qb[:, h * HD:(h + 1) * HD], cosb_ref[...], sinb_ref[...], B_HEAD_DIM // 4)
            q_ref[0, :, (A_HEADS + h) * HD:(A_HEADS + h + 1) * HD] = (y * (B_HEAD_DIM ** -0.5 * LOG2E)).astype(BF16)
        gate_ref[0] = _silu(proj(hb, G0, D)).astype(BF16)


def _inproj(x, ctx, mod, norm_g, w_in, qn_g, kn_g):
    nt = KEYS // ROWS
    xrow = lambda b, t: (b, jnp.maximum(t - 1, 0), 0)
    trow = lambda b, t: (jnp.maximum(t - 1, 0), 0)
    const2 = lambda b, t: (0, 0)
    table = pl.BlockSpec((ROWS, HD), trow)
    wb = w_in.astype(BF16)
    wvt = jnp.concatenate([wb[:, VA0:VA0 + A_KV_HEADS * HD], wb[:, VB0:VB0 + B_HEADS * HD]], axis=1).T
    return pl.pallas_call(
        _inproj_kernel,
        grid=(BATCH, nt),
        in_specs=[
            pl.BlockSpec((1, ROWS, D), xrow),
            pl.BlockSpec((1, CTX, D), lambda b, t: (b, 0, 0)),
            pl.BlockSpec((1, 3, D), lambda b, t: (b, 0, 0)),
            pl.BlockSpec((1, 3, D), lambda b, t: (BATCH, 0, 0)),
            pl.BlockSpec((1, D), const2),
            pl.BlockSpec((D, IN_COLS), const2),
            pl.BlockSpec((V_ROWS, D), const2),
            pl.BlockSpec((1, HD), const2),
            pl.BlockSpec((1, HD), const2),
            table, table, table, table,
        ],
        out_specs=[
            pl.BlockSpec((1, ROWS, K_COLS), lambda b, t: (b, t, 0)),
            pl.BlockSpec((1, 1, V_ROWS, ROWS), lambda b, t: (b, t, 0, 0)),
            pl.BlockSpec((1, ROWS, D), xrow),
            pl.BlockSpec((1, ROWS, D), xrow),
        ],
        out_shape=[
            jax.ShapeDtypeStruct((BATCH, KEYS, K_COLS), BF16),
            jax.ShapeDtypeStruct((BATCH, nt, V_ROWS, ROWS), BF16),
            jax.ShapeDtypeStruct((BATCH, SEQ, D), BF16),
            jax.ShapeDtypeStruct((BATCH, SEQ, D), BF16),
        ],
        compiler_params=pltpu.CompilerParams(
            dimension_semantics=("arbitrary", "arbitrary"), vmem_limit_bytes=VMEM_LIMIT),
        name="inproj",
    )(x, ctx, mod, mod, norm_g.reshape(1, D), wb, wvt, qn_g.reshape(1, HD), kn_g.reshape(1, HD),
      _COS_A, _SIN_A, _COS_B, _SIN_B)


N_CHAINS = A_KV_HEADS + B_HEADS
ACC_ROWS = HD + 16
AHEAD = 2
CHUNKS_PER_ITER = 4


def _attn_kernel(lq1_ref, lk1_ref, lq2_ref, lk2_ref, subln_ref, q_ref, k_ref, vt_ref, gate_ref, o_ref,
                 acc_ref, m_ref, *, lam_init):
    nq = q_ref.shape[1]
    group = A_HEADS // A_KV_HEADS
    lane = lax.broadcasted_iota(jnp.int32, (nq, HD), 1)

    q2 = []
    for kh in range(A_KV_HEADS):
        q2.append(jnp.concatenate([q_ref[0, :, (kh * group + g) * HD:(kh * group + g + 1) * HD]
                                   for g in range(group)], axis=0))
    for h in range(B_HEADS):
        qh = q_ref[0, :, (A_HEADS + h) * HD:(A_HEADS + h + 1) * HD]
        zero = jnp.zeros_like(qh)
        q2.append(jnp.concatenate([jnp.where(lane < B_HEAD_DIM, qh, zero),
                                   jnp.where(lane >= B_HEAD_DIM, qh, zero)], axis=0))

    m_ref[...] = jnp.full(m_ref.shape, -jnp.inf, F32)
    acc_ref[...] = jnp.zeros(acc_ref.shape, F32)
    ones = jnp.concatenate([jnp.ones((1, TK), BF16), jnp.zeros((ACC_ROWS - HD - 1, TK), BF16)], axis=0)

    n_chunks = KEYS // TK

    def scores(c, ci):
        r0 = pl.multiple_of(c * TK, TK)
        k = k_ref[0, pl.ds(r0, TK), ci * HD:(ci + 1) * HD]
        return lax.dot_general(k, q2[ci], (((1,), (1,)), ((), ())), preferred_element_type=F32)

    def body(c, ahead):
        c_next = jnp.minimum(c + 1, n_chunks - 1)
        sts = list(ahead)
        for ci in range(N_CHAINS):
            nxt = ci + AHEAD
            sts.append(scores(c, nxt) if nxt < N_CHAINS else scores(c_next, nxt - N_CHAINS))
            st = sts[ci]
            vt = jnp.concatenate([vt_ref[0, c, ci * HD:(ci + 1) * HD, :], ones], axis=0)
            m_old = m_ref[ci]
            m_new = jnp.maximum(m_old, jnp.max(st, axis=0, keepdims=True))
            alpha = jnp.exp2(m_old - m_new)
            pt = jnp.exp2(st - m_new).astype(BF16)
            acc_ref[ci] = alpha * acc_ref[ci] + jnp.dot(vt, pt, preferred_element_type=F32)
            m_ref[ci] = m_new
        return tuple(sts[N_CHAINS:])

    def chunks(i, a):
        for j in range(CHUNKS_PER_ITER):
            a = body(CHUNKS_PER_ITER * i + j + 1, a)
        return a

    ahead = body(0, tuple(scores(0, ci) for ci in range(AHEAD)))
    lax.fori_loop(0, (n_chunks - 1) // CHUNKS_PER_ITER, chunks, ahead)

    def normalized(ci):
        acc = acc_ref[ci]
        return acc[0:HD] * (1.0 / acc[HD:HD + 1])

    for kh in range(A_KV_HEADS):
        ot = normalized(kh)
        for g in range(group):
            h = kh * group + g
            o = ot[:, g * nq:(g + 1) * nq].T
            o_ref[0, :, h * HD:(h + 1) * HD] = (o * gate_ref[0, :, h * HD:(h + 1) * HD]).astype(BF16)

    lam = (jnp.exp(jnp.sum(lq1_ref[...] * lk1_ref[...], keepdims=True))
           - jnp.exp(jnp.sum(lq2_ref[...] * lk2_ref[...], keepdims=True)) + lam_init)
    for h in range(B_HEADS):
        c0 = (A_HEADS + h) * HD
        ot = normalized(A_KV_HEADS + h)
        od = ot[:, 0:nq] - lam * ot[:, nq:2 * nq]
        od = od * lax.rsqrt(jnp.mean(od * od, axis=0, keepdims=True) + EPS)
        o = od.T * subln_ref[...] * (1.0 - lam_init)
        o_ref[0, :, c0:c0 + HD] = (o * gate_ref[0, :, c0:c0 + HD]).astype(BF16)


def _attention(q, k, vt, gate, lq1, lk1, lq2, lk2, subln_g, lam_init):
    const2 = lambda b, i: (0, 0)
    lamspec = pl.BlockSpec((1, B_HEAD_DIM), const2)
    qspec = pl.BlockSpec((1, TQ, D), lambda b, i: (b, i, 0))
    return pl.pallas_call(
        functools.partial(_attn_kernel, lam_init=lam_init),
        grid=(BATCH, SEQ // TQ),
        in_specs=[lamspec, lamspec, lamspec, lamspec,
                  pl.BlockSpec((1, HD), const2),
                  qspec,
                  pl.BlockSpec((1, KEYS, K_COLS), lambda b, i: (b, 0, 0)),
                  pl.BlockSpec((1, KEYS // TK, V_ROWS, TK), lambda b, i: (b, 0, 0, 0)),
                  qspec],
        out_specs=qspec,
        out_shape=jax.ShapeDtypeStruct((BATCH, SEQ, D), BF16),
        scratch_shapes=[pltpu.VMEM((N_CHAINS, ACC_ROWS, 2 * TQ), F32),
                        pltpu.VMEM((N_CHAINS, 1, 2 * TQ), F32)],
        compiler_params=pltpu.CompilerParams(
            dimension_semantics=("arbitrary", "arbitrary"), vmem_limit_bytes=VMEM_LIMIT),
        name="attn",
    )(lq1.reshape(1, -1), lk1.reshape(1, -1), lq2.reshape(1, -1), lk2.reshape(1, -1),
      subln_g.reshape(1, HD), q, k, vt, gate)


def _channel_dft():
    c = np.arange(F_GROUP_DIM)
    ang = 2.0 * np.pi * ((c[:, None] * c[None, :]) % F_GROUP_DIM) / F_GROUP_DIM
    return (np.concatenate([np.cos(ang), np.sin(ang)], axis=1) / 16.0).astype(np.float32)


_CHANNEL_DFT = _channel_dft()


def _mid_kernel(x_ref, og_ref, mod0_ref, mod1_ref, g_ref, wo_ref, wf_ref, cs_ref, x1_ref, ab_ref, sg_ref):
    x1 = x_ref[0] + mod0_ref[0, 2:3, :] * jnp.dot(og_ref[0], wo_ref[...], preferred_element_type=F32)
    x1_ref[0] = x1
    mod = mod1_ref[0]
    hb = (_rms_rows(x1) * (g_ref[...] * (1.0 + mod[1:2, :])) + mod[0:1, :]).astype(BF16)
    u = jnp.dot(hb, wf_ref[:, 0:D], preferred_element_type=F32).astype(BF16)
    for gi in range(F_GROUPS):
        c0 = gi * F_GROUP_DIM
        ab = jnp.dot(u[:, c0:c0 + F_GROUP_DIM], cs_ref[...], preferred_element_type=F32)
        ab_ref[0, 0, :, c0:c0 + F_GROUP_DIM] = ab[:, 0:F_GROUP_DIM].astype(BF16)
        ab_ref[0, 1, :, c0:c0 + F_GROUP_DIM] = ab[:, F_GROUP_DIM:].astype(BF16)
    sg_ref[0] = _silu(jnp.dot(hb, wf_ref[:, D:2 * D], preferred_element_type=F32)).astype(BF16)


def _mid(x, og, mod0, mod1, norm_g, w_out, w_f):
    const2 = lambda b, i: (0, 0)
    row = lambda b, i: (b, i, 0)
    modspec = pl.BlockSpec((1, 3, D), lambda b, i: (b, 0, 0))
    return pl.pallas_call(
        _mid_kernel,
        grid=(BATCH, SEQ // ROWS),
        in_specs=[pl.BlockSpec((1, ROWS, D), row), pl.BlockSpec((1, ROWS, D), row), modspec, modspec,
                  pl.BlockSpec((1, D), const2), pl.BlockSpec((D, D), const2), pl.BlockSpec((D, 2 * D), const2),
                  pl.BlockSpec((F_GROUP_DIM, 2 * F_GROUP_DIM), const2)],
        out_specs=[pl.BlockSpec((1, ROWS, D), row),
                   pl.BlockSpec((1, 2, ROWS, D), lambda b, i: (b, 0, i, 0)),
                   pl.BlockSpec((1, ROWS, D), row)],
        out_shape=[jax.ShapeDtypeStruct((BATCH, SEQ, D), F32),
                   jax.ShapeDtypeStruct((BATCH, 2, SEQ, D), BF16),
                   jax.ShapeDtypeStruct((BATCH, SEQ, D), BF16)],
        compiler_params=pltpu.CompilerParams(
            dimension_semantics=("arbitrary", "arbitrary"), vmem_limit_bytes=VMEM_LIMIT),
        name="mid",
    )(x, og, mod0, mod1, norm_g.reshape(1, D), w_out.astype(BF16), w_f.astype(BF16),
      jnp.asarray(_CHANNEL_DFT).astype(BF16))


def _position_dft():
    k = lax.broadcasted_iota(jnp.int32, (SEQ, SEQ), 0)
    n = lax.broadcasted_iota(jnp.int32, (SEQ, SEQ), 1)
    ang = ((k * n) & (SEQ - 1)).astype(F32) * (2.0 * math.pi / SEQ)
    return jnp.concatenate([jnp.cos(ang) * (1.0 / 64.0), jnp.sin(ang) * (-1.0 / 64.0)], axis=1).astype(BF16)


def _fourier_kernel(dft_ref, ab_ref, sg_ref, x1_ref, mod_ref, wo_ref, fg_ref, o_ref):
    f = jnp.dot(dft_ref[...], ab_ref[0], preferred_element_type=F32)
    fg = (f * sg_ref[0].astype(F32)).astype(BF16)
    x2 = x1_ref[0] + mod_ref[0, 2:3, :] * jnp.dot(fg, wo_ref[...], preferred_element_type=F32)
    o_ref[0] = _rms_rows(x2) * fg_ref[...]


def _fourier(dft, ab, sg, x1, mod1, w_out, final_g):
    const2 = lambda b, i: (0, 0)
    row = lambda b, i: (b, i, 0)
    return pl.pallas_call(
        _fourier_kernel,
        grid=(BATCH, SEQ // ROWS),
        in_specs=[pl.BlockSpec((ROWS, 2 * SEQ), lambda b, i: (i, 0)),
                  pl.BlockSpec((1, 2 * SEQ, D), lambda b, i: (b, 0, 0), pipeline_mode=pl.Buffered(1)),
                  pl.BlockSpec((1, ROWS, D), row), pl.BlockSpec((1, ROWS, D), row),
                  pl.BlockSpec((1, 3, D), lambda b, i: (b, 0, 0)),
                  pl.BlockSpec((D, D), const2), pl.BlockSpec((1, D), const2)],
        out_specs=pl.BlockSpec((1, ROWS, D), row),
        out_shape=jax.ShapeDtypeStruct((BATCH, SEQ, D), F32),
        compiler_params=pltpu.CompilerParams(
            dimension_semantics=("arbitrary", "arbitrary"), vmem_limit_bytes=VMEM_LIMIT),
        name="fourier",
    )(dft, ab, sg, x1, mod1, w_out.astype(BF16), final_g.reshape(1, D))


def kernel(x, c, ctx, c_ctx, ada_w, ada_b, norm_g, attn_in_w, attn_qn_g, attn_kn_g, lam_q1, lam_k1, lam_q2, lam_k2,
           attn_subln_g, attn_out_w, fourier_in_w, fourier_out_w, final_g):
    assert x.shape == (BATCH, SEQ, D) and ctx.shape == (BATCH, CTX, D)
    assert ada_w.shape[0] == 2 and attn_in_w.shape == (1, D, IN_COLS)
    cond = jnp.concatenate([c, c_ctx[None, :], jnp.zeros((16 - BATCH - 1, D), F32)], axis=0)
    mods = _ada(cond, ada_w, ada_b)
    mod0 = mods[0].reshape(16, 3, D)
    mod1 = mods[1].reshape(16, 3, D)
    lam_init = 0.8 - 0.6 * math.exp(-0.3 * 0)

    k, vt, q, gate = _inproj(x, ctx, mod0, norm_g[0], attn_in_w[0], attn_qn_g[0], attn_kn_g[0])
    og = _attention(q, k, vt, gate, lam_q1[0], lam_k1[0], lam_q2[0], lam_k2[0], attn_subln_g[0], lam_init)
    x1, ab, sg = _mid(x, og, mod0, mod1, norm_g[1], attn_out_w[0], fourier_in_w[0])
    return _fourier(_position_dft(), ab.reshape(BATCH, 2 * SEQ, D), sg, x1, mod1, fourier_out_w[0], final_g)
```

```python
import functools
import math

import numpy as np
import jax
import jax.numpy as jnp
from jax import lax
from jax.experimental import pallas as pl
from jax.experimental.pallas import tpu as pltpu

F32 = jnp.float32
BF16 = jnp.bfloat16

D = 1024
BATCH = 8
SEQ = 4096
CTX = 256
KEYS = CTX + SEQ
GRID_W = 64
ROPE_THETA = 10000.0
EPS = 1e-6
LOG2E = math.log2(math.e)

HD = 128
A_HEADS = 4
A_KV_HEADS = 2
B_HEADS = 4
B_HEAD_DIM = 64
KA0, VA0, KB0, VB0 = 0, 256, 512, 1024
K_COLS = 768
V_ROWS = 768
QA0, QB0, G0 = 1536, 2048, 2560
IN_COLS = 3584
F_GROUPS = 4
F_GROUP_DIM = 256

ROWS = 256
TQ = 256
TK = 256
VMEM_LIMIT = 56 * 1024 * 1024


def _silu(v):
    return v * (1.0 / (1.0 + jnp.exp(-v)))


def _rms_rows(v):
    return v * lax.rsqrt(jnp.mean(v * v, axis=-1, keepdims=True) + EPS)


def _ada_kernel(c_ref, w_ref, b_ref, o_ref):
    s = _silu(c_ref[...]).astype(BF16)
    o_ref[0] = jnp.dot(s, w_ref[0].astype(BF16), preferred_element_type=F32) + b_ref[0]


def _ada(cond, ada_w, ada_b):
    depth, _, n3 = ada_w.shape
    tn = 768
    return pl.pallas_call(
        _ada_kernel,
        grid=(depth, n3 // tn),
        in_specs=[
            pl.BlockSpec((16, D), lambda l, j: (0, 0)),
            pl.BlockSpec((1, D, tn), lambda l, j: (l, 0, j)),
            pl.BlockSpec((1, 1, tn), lambda l, j: (l, 0, j)),
        ],
        out_specs=pl.BlockSpec((1, 16, tn), lambda l, j: (l, 0, j)),
        out_shape=jax.ShapeDtypeStruct((depth, 16, n3), F32),
        compiler_params=pltpu.CompilerParams(
            dimension_semantics=("arbitrary", "arbitrary"), vmem_limit_bytes=VMEM_LIMIT),
        name="ada",
    )(cond, ada_w, ada_b.reshape(depth, 1, n3))


def _rope_table(dim):
    q4 = dim // 4
    pos = np.arange(SEQ)
    axis_pos = np.stack([pos // GRID_W, pos % GRID_W], axis=1).astype(np.float64)
    inv = ROPE_THETA ** (-np.arange(q4, dtype=np.float64) / q4)
    ang = axis_pos[:, :, None, None] * inv[None, None, None, :]
    ang = np.broadcast_to(ang, (SEQ, 2, 2, q4))
    sign = np.array([-1.0, 1.0]).reshape(1, 1, 2, 1)
    cos = np.cos(ang).reshape(SEQ, dim)
    sin = (np.sin(ang) * sign).reshape(SEQ, dim)
    return cos.astype(np.float32), sin.astype(np.float32)


_COS_A, _SIN_A = _rope_table(HD)
_cb, _sb = _rope_table(B_HEAD_DIM)
_COS_B, _SIN_B = np.tile(_cb, (1, 2)), np.tile(_sb, (1, 2))


def _rope(y, cos, sin, half):
    lane = lax.broadcasted_iota(jnp.int32, y.shape, 1)
    first = (lane & (2 * half - 1)) < half
    partner = jnp.where(first, pltpu.roll(y, HD - half, 1), pltpu.roll(y, half, 1))
    return y * cos + partner * sin


def _inproj_kernel(x_ref, ctx_ref, modx_ref, modc_ref, g_ref, w_ref, wvt_ref, qn_ref, kn_ref,
                   cosa_ref, sina_ref, cosb_ref, sinb_ref, k_ref, vt_ref, q_ref, gate_ref):
    t = pl.program_id(1)

    def hidden(src, mod):
        a = g_ref[...] * (1.0 + mod[1:2, :])
        return (_rms_rows(src) * a + mod[0:1, :]).astype(BF16)

    def proj(hb, c0, width):
        return jnp.dot(hb, w_ref[:, c0:c0 + width], preferred_element_type=F32)

    def keys_values(hb, rope):
        ka = proj(hb, KA0, A_KV_HEADS * HD)
        for h in range(A_KV_HEADS):
            y = _rms_rows(ka[:, h * HD:(h + 1) * HD]) * kn_ref[...]
            if rope:
                y = _rope(y, cosa_ref[...], sina_ref[...], HD // 4)
            k_ref[0, :, h * HD:(h + 1) * HD] = y.astype(BF16)
        kb = proj(hb, KB0, B_HEADS * HD)
        for h in range(B_HEADS):
            y = kb[:, h * HD:(h + 1) * HD]
            if rope:
                y = _rope(y, cosb_ref[...], sinb_ref[...], B_HEAD_DIM // 4)
            k_ref[0, :, (A_KV_HEADS + h) * HD:(A_KV_HEADS + h + 1) * HD] = y.astype(BF16)
        vt = lax.dot_general(wvt_ref[...], hb, (((1,), (1,)), ((), ())), preferred_element_type=F32)
        vt_ref[0, 0] = vt.astype(BF16)

    @pl.when(t == 0)
    def _():
        keys_values(hidden(ctx_ref[0], modc_ref[0]), rope=False)

    @pl.when(t > 0)
    def _():
        hb = hidden(x_ref[0], modx_ref[0])
        keys_values(hb, rope=True)
        qa = proj(hb, QA0, A_HEADS * HD)
        qgain = qn_ref[...] * (HD ** -0.5 * LOG2E)
        for h in range(A_HEADS):
            y = _rms_rows(qa[:, h * HD:(h + 1) * HD]) * qgain
            y = _rope(y, cosa_ref[...], sina_ref[...], HD // 4)
            q_ref[0, :, h * HD:(h + 1) * HD] = y.astype(BF16)
        qb = proj(hb, QB0, B_HEADS * HD)
        for h in range(B_HEADS):
            y = _rope(qb[:, h * HD:(h + 1) * HD], cosb_ref[...], sinb_ref[...], B_HEAD_DIM // 4)
            q_ref[0, :, (A_HEADS + h) * HD:(A_HEADS + h + 1) * HD] = (y * (B_HEAD_DIM ** -0.5 * LOG2E)).astype(BF16)
        gate_ref[0] = _silu(proj(hb, G0, D)).astype(BF16)


def _inproj(x, ctx, mod, norm_g, w_in, qn_g, kn_g):
    nt = KEYS // ROWS
    xrow = lambda b, t: (b, jnp.maximum(t - 1, 0), 0)
    trow = lambda b, t: (jnp.maximum(t - 1, 0), 0)
    const2 = lambda b, t: (0, 0)
    table = pl.BlockSpec((ROWS, HD), trow)
    wb = w_in.astype(BF16)
    wvt = jnp.concatenate([wb[:, VA0:VA0 + A_KV_HEADS * HD], wb[:, VB0:VB0 + B_HEADS * HD]], axis=1).T
    return pl.pallas_call(
        _inproj_kernel,
        grid=(BATCH, nt),
        in_specs=[
            pl.BlockSpec((1, ROWS, D), xrow),
            pl.BlockSpec((1, CTX, D), lambda b, t: (b, 0, 0)),
            pl.BlockSpec((1, 3, D), lambda b, t: (b, 0, 0)),
            pl.BlockSpec((1, 3, D), lambda b, t: (BATCH, 0, 0)),
            pl.BlockSpec((1, D), const2),
            pl.BlockSpec((D, IN_COLS), const2),
            pl.BlockSpec((V_ROWS, D), const2),
            pl.BlockSpec((1, HD), const2),
            pl.BlockSpec((1, HD), const2),
            table, table, table, table,
        ],
        out_specs=[
            pl.BlockSpec((1, ROWS, K_COLS), lambda b, t: (b, t, 0)),
            pl.BlockSpec((1, 1, V_ROWS, ROWS), lambda b, t: (b, t, 0, 0)),
            pl.BlockSpec((1, ROWS, D), xrow),
            pl.BlockSpec((1, ROWS, D), xrow),
        ],
        out_shape=[
            jax.ShapeDtypeStruct((BATCH, KEYS, K_COLS), BF16),
            jax.ShapeDtypeStruct((BATCH, nt, V_ROWS, ROWS), BF16),
            jax.ShapeDtypeStruct((BATCH, SEQ, D), BF16),
            jax.ShapeDtypeStruct((BATCH, SEQ, D), BF16),
        ],
        compiler_params=pltpu.CompilerParams(
            dimension_semantics=("arbitrary", "arbitrary"), vmem_limit_bytes=VMEM_LIMIT),
        name="inproj",
    )(x, ctx, mod, mod, norm_g.reshape(1, D), wb, wvt, qn_g.reshape(1, HD), kn_g.reshape(1, HD),
      _COS_A, _SIN_A, _COS_B, _SIN_B)


N_CHAINS = A_KV_HEADS + B_HEADS
ACC_ROWS = HD + 16
AHEAD = 2
CHUNKS_PER_ITER = 8


def _attn_kernel(lq1_ref, lk1_ref, lq2_ref, lk2_ref, subln_ref, q_ref, k_ref, vt_ref, gate_ref, o_ref,
                 acc_ref, m_ref, *, lam_init):
    nq = q_ref.shape[1]
    group = A_HEADS // A_KV_HEADS
    lane = lax.broadcasted_iota(jnp.int32, (nq, HD), 1)

    q2 = []
    for kh in range(A_KV_HEADS):
        q2.append(jnp.concatenate([q_ref[0, :, (kh * group + g) * HD:(kh * group + g + 1) * HD]
                                   for g in range(group)], axis=0))
    for h in range(B_HEADS):
        qh = q_ref[0, :, (A_HEADS + h) * HD:(A_HEADS + h + 1) * HD]
        zero = jnp.zeros_like(qh)
        q2.append(jnp.concatenate([jnp.where(lane < B_HEAD_DIM, qh, zero),
                                   jnp.where(lane >= B_HEAD_DIM, qh, zero)], axis=0))

    m_ref[...] = jnp.full(m_ref.shape, -jnp.inf, F32)
    acc_ref[...] = jnp.zeros(acc_ref.shape, F32)
    ones = jnp.concatenate([jnp.ones((1, TK), BF16), jnp.zeros((ACC_ROWS - HD - 1, TK), BF16)], axis=0)

    n_chunks = KEYS // TK

    def scores(c, ci):
        r0 = pl.multiple_of(c * TK, TK)
        k = k_ref[0, pl.ds(r0, TK), ci * HD:(ci + 1) * HD]
        return lax.dot_general(k, q2[ci], (((1,), (1,)), ((), ())), preferred_element_type=F32)

    def body(c, ahead):
        c_next = jnp.minimum(c + 1, n_chunks - 1)
        sts = list(ahead)
        for ci in range(N_CHAINS):
            nxt = ci + AHEAD
            sts.append(scores(c, nxt) if nxt < N_CHAINS else scores(c_next, nxt - N_CHAINS))
            st = sts[ci]
            vt = jnp.concatenate([vt_ref[0, c, ci * HD:(ci + 1) * HD, :], ones], axis=0)
            m_old = m_ref[ci]
            m_new = jnp.maximum(m_old, jnp.max(st, axis=0, keepdims=True))
            alpha = jnp.exp2(m_old - m_new)
            pt = jnp.exp2(st - m_new).astype(BF16)
            acc_ref[ci] = alpha * acc_ref[ci] + jnp.dot(vt, pt, preferred_element_type=F32)
            m_ref[ci] = m_new
        return tuple(sts[N_CHAINS:])

    def chunks(i, a):
        for j in range(CHUNKS_PER_ITER):
            a = body(CHUNKS_PER_ITER * i + j + 1, a)
        return a

    ahead = body(0, tuple(scores(0, ci) for ci in range(AHEAD)))
    lax.fori_loop(0, (n_chunks - 1) // CHUNKS_PER_ITER, chunks, ahead)

    def normalized(ci):
        acc = acc_ref[ci]
        return acc[0:HD] * (1.0 / acc[HD:HD + 1])

    for kh in range(A_KV_HEADS):
        ot = normalized(kh)
        for g in range(group):
            h = kh * group + g
            o = ot[:, g * nq:(g + 1) * nq].T
            o_ref[0, :, h * HD:(h + 1) * HD] = (o * gate_ref[0, :, h * HD:(h + 1) * HD]).astype(BF16)

    lam = (jnp.exp(jnp.sum(lq1_ref[...] * lk1_ref[...], keepdims=True))
           - jnp.exp(jnp.sum(lq2_ref[...] * lk2_ref[...], keepdims=True)) + lam_init)
    for h in range(B_HEADS):
        c0 = (A_HEADS + h) * HD
        ot = normalized(A_KV_HEADS + h)
        od = ot[:, 0:nq] - lam * ot[:, nq:2 * nq]
        od = od * lax.rsqrt(jnp.mean(od * od, axis=0, keepdims=True) + EPS)
        o = od.T * subln_ref[...] * (1.0 - lam_init)
        o_ref[0, :, c0:c0 + HD] = (o * gate_ref[0, :, c0:c0 + HD]).astype(BF16)


def _attention(q, k, vt, gate, lq1, lk1, lq2, lk2, subln_g, lam_init):
    const2 = lambda b, i: (0, 0)
    lamspec = pl.BlockSpec((1, B_HEAD_DIM), const2)
    qspec = pl.BlockSpec((1, TQ, D), lambda b, i: (b, i, 0))
    return pl.pallas_call(
        functools.partial(_attn_kernel, lam_init=lam_init),
        grid=(BATCH, SEQ // TQ),
        in_specs=[lamspec, lamspec, lamspec, lamspec,
                  pl.BlockSpec((1, HD), const2),
                  qspec,
                  pl.BlockSpec((1, KEYS, K_COLS), lambda b, i: (b, 0, 0)),
                  pl.BlockSpec((1, KEYS // TK, V_ROWS, TK), lambda b, i: (b, 0, 0, 0)),
                  qspec],
        out_specs=qspec,
        out_shape=jax.ShapeDtypeStruct((BATCH, SEQ, D), BF16),
        scratch_shapes=[pltpu.VMEM((N_CHAINS, ACC_ROWS, 2 * TQ), F32),
                        pltpu.VMEM((N_CHAINS, 1, 2 * TQ), F32)],
        compiler_params=pltpu.CompilerParams(
            dimension_semantics=("arbitrary", "arbitrary"), vmem_limit_bytes=VMEM_LIMIT),
        name="attn",
    )(lq1.reshape(1, -1), lk1.reshape(1, -1), lq2.reshape(1, -1), lk2.reshape(1, -1),
      subln_g.reshape(1, HD), q, k, vt, gate)


def _channel_dft():
    c = np.arange(F_GROUP_DIM)
    ang = 2.0 * np.pi * ((c[:, None] * c[None, :]) % F_GROUP_DIM) / F_GROUP_DIM
    return (np.concatenate([np.cos(ang), np.sin(ang)], axis=1) / 16.0).astype(np.float32)


_CHANNEL_DFT = _channel_dft()


def _mid_kernel(x_ref, og_ref, mod0_ref, mod1_ref, g_ref, wo_ref, wf_ref, cs_ref, x1_ref, ab_ref, sg_ref):
    x1 = x_ref[0] + mod0_ref[0, 2:3, :] * jnp.dot(og_ref[0], wo_ref[...], preferred_element_type=F32)
    x1_ref[0] = x1
    mod = mod1_ref[0]
    hb = (_rms_rows(x1) * (g_ref[...] * (1.0 + mod[1:2, :])) + mod[0:1, :]).astype(BF16)
    u = jnp.dot(hb, wf_ref[:, 0:D], preferred_element_type=F32).astype(BF16)
    for gi in range(F_GROUPS):
        c0 = gi * F_GROUP_DIM
        ab = jnp.dot(u[:, c0:c0 + F_GROUP_DIM], cs_ref[...], preferred_element_type=F32)
        ab_ref[0, 0, :, c0:c0 + F_GROUP_DIM] = ab[:, 0:F_GROUP_DIM].astype(BF16)
        ab_ref[0, 1, :, c0:c0 + F_GROUP_DIM] = ab[:, F_GROUP_DIM:].astype(BF16)
    sg_ref[0] = _silu(jnp.dot(hb, wf_ref[:, D:2 * D], preferred_element_type=F32)).astype(BF16)


def _mid(x, og, mod0, mod1, norm_g, w_out, w_f):
    const2 = lambda b, i: (0, 0)
    row = lambda b, i: (b, i, 0)
    modspec = pl.BlockSpec((1, 3, D), lambda b, i: (b, 0, 0))
    return pl.pallas_call(
        _mid_kernel,
        grid=(BATCH, SEQ // ROWS),
        in_specs=[pl.BlockSpec((1, ROWS, D), row), pl.BlockSpec((1, ROWS, D), row), modspec, modspec,
                  pl.BlockSpec((1, D), const2), pl.BlockSpec((D, D), const2), pl.BlockSpec((D, 2 * D), const2),
                  pl.BlockSpec((F_GROUP_DIM, 2 * F_GROUP_DIM), const2)],
        out_specs=[pl.BlockSpec((1, ROWS, D), row),
                   pl.BlockSpec((1, 2, ROWS, D), lambda b, i: (b, 0, i, 0)),
                   pl.BlockSpec((1, ROWS, D), row)],
        out_shape=[jax.ShapeDtypeStruct((BATCH, SEQ, D), F32),
                   jax.ShapeDtypeStruct((BATCH, 2, SEQ, D), BF16),
                   jax.ShapeDtypeStruct((BATCH, SEQ, D), BF16)],
        compiler_params=pltpu.CompilerParams(
            dimension_semantics=("arbitrary", "arbitrary"), vmem_limit_bytes=VMEM_LIMIT),
        name="mid",
    )(x, og, mod0, mod1, norm_g.reshape(1, D), w_out.astype(BF16), w_f.astype(BF16),
      jnp.asarray(_CHANNEL_DFT).astype(BF16))


RADIX = 16
FBLK = RADIX * RADIX
STAGE_BLOCKS = 2
STAGE_STEPS = RADIX // STAGE_BLOCKS


def _stage_matrices():
    w = lambda e: np.exp(-2j * np.pi * (e % SEQ) / SEQ)
    i = np.arange(RADIX)
    eye = np.eye(RADIX)
    g = np.einsum("cd,kac->dkac", eye, w(256 * i[:, None, None] * i[None, :, None] + i[:, None, None] * i[None, None, :]))
    g = g.reshape(FBLK, FBLK)
    m1 = np.block([[g.real, g.imag], [g.imag, -g.real]])
    h = np.einsum("jk,qbk->qjbk", eye, w(256 * i[:, None, None] * i[None, :, None] + 16 * i[None, None, :] * i[None, :, None]))
    h = h.reshape(FBLK, FBLK)
    m2 = np.block([[h.real, -h.imag], [h.imag, h.real]])
    ph = w(i[None, None, :] * (16 * i[:, None, None] + 256 * i[None, :, None]))
    e = np.einsum("jk,bqc->bqjck", eye, ph).reshape(RADIX, FBLK, FBLK)
    m3 = np.concatenate([e.real, -e.imag], axis=2) / 64.0
    return m1.astype(np.float32), m2.astype(np.float32), m3.astype(np.float32)


_M1, _M2, _M3 = _stage_matrices()


def _fourier_kernel(ab_ref, m1_ref, m2_ref, m3_ref, sg_ref, x1_ref, mod_ref, wo_ref, fg_ref, o_ref, z_ref):
    s = pl.program_id(1)

    @pl.when(s < STAGE_STEPS)
    def _():
        for i in range(STAGE_BLOCKS):
            b = s * STAGE_BLOCKS + i
            rhs = ab_ref[0, :, :, i].reshape(2 * FBLK, D)
            y = jnp.dot(m1_ref[...], rhs, preferred_element_type=F32).astype(BF16)
            z_ref[:, :, pl.ds(b, 1)] = y.reshape(2, RADIX, 1, RADIX, D)

    @pl.when((s >= STAGE_STEPS) & (s < 2 * STAGE_STEPS))
    def _():
        for i in range(STAGE_BLOCKS):
            c = (s - STAGE_STEPS) * STAGE_BLOCKS + i
            rhs = z_ref[:, pl.ds(c, 1)].reshape(2 * FBLK, D)
            y = jnp.dot(m2_ref[...], rhs, preferred_element_type=F32).astype(BF16)
            z_ref[:, pl.ds(c, 1)] = y.reshape(2, 1, RADIX, RADIX, D)

    @pl.when(s >= 2 * STAGE_STEPS)
    def _():
        for i in range(STAGE_BLOCKS):
            kb = (s - 2 * STAGE_STEPS) * STAGE_BLOCKS + i
            rhs = z_ref[:, :, pl.ds(kb, 1)].reshape(2 * FBLK, D)
            f = jnp.dot(m3_ref[i], rhs, preferred_element_type=F32)
            fg = (f * sg_ref[0, :, i].reshape(FBLK, D).astype(F32)).astype(BF16)
            x2 = x1_ref[0, :, i].reshape(FBLK, D) + mod_ref[0, 2:3, :] * jnp.dot(fg, wo_ref[...], preferred_element_type=F32)
            o_ref[0, :, i] = (_rms_rows(x2) * fg_ref[...]).reshape(RADIX, RADIX, D)


def _fourier(ab, sg, x1, mod1, w_out, final_g):
    digits = (RADIX, RADIX, RADIX)
    const2 = lambda b, s: (0, 0)
    last = lambda s: jnp.clip(s - 2 * STAGE_STEPS, 0, STAGE_STEPS - 1)
    rows3 = pl.BlockSpec((1, RADIX, STAGE_BLOCKS, RADIX, D), lambda b, s: (b, 0, last(s), 0, 0))
    out = pl.pallas_call(
        _fourier_kernel,
        grid=(BATCH, 3 * STAGE_STEPS),
        in_specs=[pl.BlockSpec((1, 2, RADIX, STAGE_BLOCKS, RADIX, D),
                               lambda b, s: (b, 0, 0, jnp.minimum(s, STAGE_STEPS - 1), 0, 0)),
                  pl.BlockSpec((2 * FBLK, 2 * FBLK), const2),
                  pl.BlockSpec((2 * FBLK, 2 * FBLK), const2),
                  pl.BlockSpec((STAGE_BLOCKS, FBLK, 2 * FBLK), lambda b, s: (last(s), 0, 0)),
                  rows3, rows3,
                  pl.BlockSpec((1, 3, D), lambda b, s: (b, 0, 0)),
                  pl.BlockSpec((D, D), const2), pl.BlockSpec((1, D), const2)],
        out_specs=rows3,
        out_shape=jax.ShapeDtypeStruct((BATCH,) + digits + (D,), F32),
        scratch_shapes=[pltpu.VMEM((2,) + digits + (D,), BF16)],
        compiler_params=pltpu.CompilerParams(
            dimension_semantics=("arbitrary", "arbitrary"), vmem_limit_bytes=VMEM_LIMIT),
        name="fourier",
    )(ab.reshape((BATCH, 2) + digits + (D,)), jnp.asarray(_M1).astype(BF16), jnp.asarray(_M2).astype(BF16),
      jnp.asarray(_M3).astype(BF16), sg.reshape((BATCH,) + digits + (D,)), x1.reshape((BATCH,) + digits + (D,)),
      mod1, w_out.astype(BF16), final_g.reshape(1, D))
    return out.reshape(BATCH, SEQ, D)


def kernel(x, c, ctx, c_ctx, ada_w, ada_b, norm_g, attn_in_w, attn_qn_g, attn_kn_g, lam_q1, lam_k1, lam_q2, lam_k2,
           attn_subln_g, attn_out_w, fourier_in_w, fourier_out_w, final_g):
    assert x.shape == (BATCH, SEQ, D) and ctx.shape == (BATCH, CTX, D)
    assert ada_w.shape[0] == 2 and attn_in_w.shape == (1, D, IN_COLS)
    cond = jnp.concatenate([c, c_ctx[None, :], jnp.zeros((16 - BATCH - 1, D), F32)], axis=0)
    mods = _ada(cond, ada_w, ada_b)
    mod0 = mods[0].reshape(16, 3, D)
    mod1 = mods[1].reshape(16, 3, D)
    lam_init = 0.8 - 0.6 * math.exp(-0.3 * 0)

    k, vt, q, gate = _inproj(x, ctx, mod0, norm_g[0], attn_in_w[0], attn_qn_g[0], attn_kn_g[0])
    og = _attention(q, k, vt, gate, lam_q1[0], lam_k1[0], lam_q2[0], lam_k2[0], attn_subln_g[0], lam_init)
    x1, ab, sg = _mid(x, og, mod0, mod1, norm_g[1], attn_out_w[0], fourier_in_w[0])
    return _fourier(ab, sg, x1, mod1, fourier_out_w[0], final_g)
```

```python
import functools
import math

import numpy as np
import jax
import jax.numpy as jnp
from jax import lax
from jax.experimental import pallas as pl
from jax.experimental.pallas import tpu as pltpu

F32 = jnp.float32
BF16 = jnp.bfloat16

D = 1024
BATCH = 8
SEQ = 4096
CTX = 256
KEYS = CTX + SEQ
GRID_W = 64
ROPE_THETA = 10000.0
EPS = 1e-6
LOG2E = math.log2(math.e)

HD = 128
A_HEADS = 4
A_KV_HEADS = 2
B_HEADS = 4
B_HEAD_DIM = 64
KA0, VA0, KB0, VB0 = 0, 256, 512, 1024
K_COLS = 768
V_ROWS = 768
QA0, QB0, G0 = 1536, 2048, 2560
IN_COLS = 3584
F_GROUPS = 4
F_GROUP_DIM = 256

ROWS = 512
MID_ROWS = 512
TQ = 256
TK = 256
VMEM_LIMIT = 56 * 1024 * 1024


def _silu(v):
    return v * (1.0 / (1.0 + jnp.exp(-v)))


def _rms_rows(v):
    return v * lax.rsqrt(jnp.mean(v * v, axis=-1, keepdims=True) + EPS)


def _ada_kernel(c_ref, w_ref, b_ref, o_ref):
    s = _silu(c_ref[...]).astype(BF16)
    o_ref[0] = jnp.dot(s, w_ref[0].astype(BF16), preferred_element_type=F32) + b_ref[0]


def _ada(cond, ada_w, ada_b):
    depth, _, n3 = ada_w.shape
    tn = 768
    return pl.pallas_call(
        _ada_kernel,
        grid=(depth, n3 // tn),
        in_specs=[
            pl.BlockSpec((16, D), lambda l, j: (0, 0)),
            pl.BlockSpec((1, D, tn), lambda l, j: (l, 0, j)),
            pl.BlockSpec((1, 1, tn), lambda l, j: (l, 0, j)),
        ],
        out_specs=pl.BlockSpec((1, 16, tn), lambda l, j: (l, 0, j)),
        out_shape=jax.ShapeDtypeStruct((depth, 16, n3), F32),
        compiler_params=pltpu.CompilerParams(
            dimension_semantics=("arbitrary", "arbitrary"), vmem_limit_bytes=VMEM_LIMIT),
        name="ada",
    )(cond, ada_w, ada_b.reshape(depth, 1, n3))


def _rope_table(dim, maps):
    q4 = dim // 4
    pos = np.arange(SEQ)
    axis_pos = np.stack([pos // GRID_W, pos % GRID_W], axis=1).astype(np.float64)
    inv = ROPE_THETA ** (-np.arange(q4, dtype=np.float64) / q4)
    ang = axis_pos[:, None, None, :, None] * inv[None, None, None, None, :]
    ang = np.broadcast_to(ang, (SEQ, 2, maps, 2, q4))
    sign = np.array([-1.0, 1.0]).reshape(1, 2, 1, 1, 1)
    cos = np.cos(ang).reshape(SEQ, HD)
    sin = (np.sin(ang) * sign).reshape(SEQ, HD)
    return cos.astype(np.float32), sin.astype(np.float32)


_COS_A, _SIN_A = _rope_table(HD, 1)
_COS_B, _SIN_B = _rope_table(B_HEAD_DIM, 2)


def _rope_order(w, heads, maps):
    q4 = HD // maps // 4
    lead = w.shape[:-1]
    w = w.reshape(lead + (heads, maps, 2, 2, q4))
    n = len(lead)
    w = w.transpose(tuple(range(n)) + (n, n + 3, n + 1, n + 2, n + 4))
    return w.reshape(lead + (heads * HD,))


def _rope(y, cos, sin):
    return y * cos + pltpu.roll(y, HD // 2, 1) * sin


def _hidden(src, g_ref, mod):
    a = g_ref[...] * (1.0 + mod[1:2, :])
    return (_rms_rows(src) * a + mod[0:1, :]).astype(BF16)


def _keys_values(hb, w_ref, wvt_ref, kn_ref, k_ref, rope):
    ka = jnp.dot(hb, w_ref[:, KA0:KA0 + A_KV_HEADS * HD], preferred_element_type=F32)
    for h in range(A_KV_HEADS):
        y = _rms_rows(ka[:, h * HD:(h + 1) * HD]) * kn_ref[...]
        if rope is not None:
            y = _rope(y, rope[0], rope[1])
        k_ref[0, :, h * HD:(h + 1) * HD] = y.astype(BF16)
    kb = jnp.dot(hb, w_ref[:, KB0:KB0 + B_HEADS * HD], preferred_element_type=F32)
    for h in range(B_HEADS):
        y = kb[:, h * HD:(h + 1) * HD]
        if rope is not None:
            y = _rope(y, rope[2], rope[3])
        k_ref[0, :, (A_KV_HEADS + h) * HD:(A_KV_HEADS + h + 1) * HD] = y.astype(BF16)
    return lax.dot_general(wvt_ref[...], hb, (((1,), (1,)), ((), ())), preferred_element_type=F32).astype(BF16)


def _ctxproj_kernel(ctx_ref, mod_ref, g_ref, w_ref, wvt_ref, kn_ref, k_ref, vt_ref):
    hb = _hidden(ctx_ref[0], g_ref, mod_ref[0])
    vt_ref[0] = _keys_values(hb, w_ref, wvt_ref, kn_ref, k_ref, None)


def _inproj_kernel(x_ref, mod_ref, g_ref, w_ref, wvt_ref, qn_ref, kn_ref,
                   cosa_ref, sina_ref, cosb_ref, sinb_ref, k_ref, vt_ref, q_ref, gate_ref):
    hb = _hidden(x_ref[0], g_ref, mod_ref[0])
    rope = (cosa_ref[...], sina_ref[...], cosb_ref[...], sinb_ref[...])
    vt = _keys_values(hb, w_ref, wvt_ref, kn_ref, k_ref, rope)
    for j in range(ROWS // TK):
        vt_ref[0, j] = vt[:, j * TK:(j + 1) * TK]
    qa = jnp.dot(hb, w_ref[:, QA0:QA0 + A_HEADS * HD], preferred_element_type=F32)
    qgain = qn_ref[...] * (HD ** -0.5 * LOG2E)
    for h in range(A_HEADS):
        y = _rope(_rms_rows(qa[:, h * HD:(h + 1) * HD]) * qgain, rope[0], rope[1])
        q_ref[0, :, h * HD:(h + 1) * HD] = y.astype(BF16)
    qb = jnp.dot(hb, w_ref[:, QB0:QB0 + B_HEADS * HD], preferred_element_type=F32)
    for h in range(B_HEADS):
        y = _rope(qb[:, h * HD:(h + 1) * HD], rope[2], rope[3])
        q_ref[0, :, (A_HEADS + h) * HD:(A_HEADS + h + 1) * HD] = (y * (B_HEAD_DIM ** -0.5 * LOG2E)).astype(BF16)
    gate_ref[0] = _silu(jnp.dot(hb, w_ref[:, G0:G0 + D], preferred_element_type=F32)).astype(BF16)


def _inproj(x, ctx, mod, norm_g, w_in, qn_g, kn_g):
    const2 = lambda b, t: (0, 0)
    wb = jnp.concatenate([
        _rope_order(w_in[:, KA0:VA0], A_KV_HEADS, 1), w_in[:, VA0:KB0],
        _rope_order(w_in[:, KB0:VB0], B_HEADS, 2), w_in[:, VB0:QA0],
        _rope_order(w_in[:, QA0:QB0], A_HEADS, 1), _rope_order(w_in[:, QB0:G0], B_HEADS, 2),
        w_in[:, G0:]], axis=1).astype(BF16)
    wvt = jnp.concatenate([w_in[:, VA0:VA0 + A_KV_HEADS * HD].T, w_in[:, VB0:VB0 + B_HEADS * HD].T],
                          axis=0).astype(BF16)
    g2, kn2 = norm_g.reshape(1, D), _rope_order(kn_g, 1, 1).reshape(1, HD)
    params = pltpu.CompilerParams(dimension_semantics=("arbitrary", "arbitrary"), vmem_limit_bytes=VMEM_LIMIT)
    k_ctx, vt_ctx = pl.pallas_call(
        _ctxproj_kernel,
        grid=(BATCH, 1),
        in_specs=[pl.BlockSpec((1, CTX, D), lambda b, t: (b, 0, 0)),
                  pl.BlockSpec((1, 3, D), lambda b, t: (BATCH, 0, 0)),
                  pl.BlockSpec((1, D), const2), pl.BlockSpec((D, IN_COLS), const2),
                  pl.BlockSpec((V_ROWS, D), const2), pl.BlockSpec((1, HD), const2)],
        out_specs=[pl.BlockSpec((1, CTX, K_COLS), lambda b, t: (b, 0, 0)),
                   pl.BlockSpec((1, V_ROWS, CTX), lambda b, t: (b, 0, 0))],
        out_shape=[jax.ShapeDtypeStruct((BATCH, CTX, K_COLS), BF16),
                   jax.ShapeDtypeStruct((BATCH, V_ROWS, CTX), BF16)],
        compiler_params=params,
        name="ctxproj",
    )(ctx, mod, g2, wb, wvt, kn2)

    row = lambda b, t: (b, t, 0)
    table = pl.BlockSpec((ROWS, HD), lambda b, t: (t, 0))
    k, vt, q, gate = pl.pallas_call(
        _inproj_kernel,
        grid=(BATCH, SEQ // ROWS),
        in_specs=[pl.BlockSpec((1, ROWS, D), row),
                  pl.BlockSpec((1, 3, D), lambda b, t: (b, 0, 0)),
                  pl.BlockSpec((1, D), const2), pl.BlockSpec((D, IN_COLS), const2),
                  pl.BlockSpec((V_ROWS, D), const2), pl.BlockSpec((1, HD), const2), pl.BlockSpec((1, HD), const2),
                  table, table, table, table],
        out_specs=[pl.BlockSpec((1, ROWS, K_COLS), row),
                   pl.BlockSpec((1, ROWS // TK, V_ROWS, TK), lambda b, t: (b, t, 0, 0)),
                   pl.BlockSpec((1, ROWS, D), row),
                   pl.BlockSpec((1, ROWS, D), row)],
        out_shape=[jax.ShapeDtypeStruct((BATCH, SEQ, K_COLS), BF16),
                   jax.ShapeDtypeStruct((BATCH, SEQ // TK, V_ROWS, TK), BF16),
                   jax.ShapeDtypeStruct((BATCH, SEQ, D), BF16),
                   jax.ShapeDtypeStruct((BATCH, SEQ, D), BF16)],
        compiler_params=params,
        name="inproj",
    )(x, mod, g2, wb, wvt, _rope_order(qn_g, 1, 1).reshape(1, HD), kn2, _COS_A, _SIN_A, _COS_B, _SIN_B)
    return k_ctx, vt_ctx, k, vt, q, gate


N_CHAINS = A_KV_HEADS + B_HEADS
ACC_ROWS = HD + 16
AHEAD = 2
CHUNKS_PER_ITER = 8


def _attn_kernel(lq1_ref, lk1_ref, lq2_ref, lk2_ref, subln_ref, q_ref, kc_ref, vtc_ref, k_ref, vt_ref, gate_ref, o_ref,
                 acc_ref, m_ref, *, lam_init):
    nq = q_ref.shape[1]
    group = A_HEADS // A_KV_HEADS
    lane = lax.broadcasted_iota(jnp.int32, (nq, HD), 1)

    q2 = []
    for kh in range(A_KV_HEADS):
        q2.append(jnp.concatenate([q_ref[0, :, (kh * group + g) * HD:(kh * group + g + 1) * HD]
                                   for g in range(group)], axis=0))
    for h in range(B_HEADS):
        qh = q_ref[0, :, (A_HEADS + h) * HD:(A_HEADS + h + 1) * HD]
        zero = jnp.zeros_like(qh)
        map1 = (lane & (HD // 4)) == 0
        q2.append(jnp.concatenate([jnp.where(map1, qh, zero), jnp.where(map1, zero, qh)], axis=0))

    ones = jnp.concatenate([jnp.ones((1, TK), BF16), jnp.zeros((ACC_ROWS - HD - 1, TK), BF16)], axis=0)

    n_chunks = SEQ // TK

    def scores(c, ci):
        if c is None:
            k = kc_ref[0, :, ci * HD:(ci + 1) * HD]
        else:
            k = k_ref[0, pl.ds(pl.multiple_of(c * TK, TK), TK), ci * HD:(ci + 1) * HD]
        return lax.dot_general(k, q2[ci], (((1,), (1,)), ((), ())), preferred_element_type=F32)

    def body(c, ahead):
        first = c is None
        c_next = 0 if first else jnp.minimum(c + 1, n_chunks - 1)
        sts = list(ahead)
        for ci in range(N_CHAINS):
            nxt = ci + AHEAD
            sts.append(scores(c, nxt) if nxt < N_CHAINS else scores(c_next, nxt - N_CHAINS))
            st = sts[ci]
            v_t = vtc_ref[0, ci * HD:(ci + 1) * HD, :] if first else vt_ref[0, c, ci * HD:(ci + 1) * HD, :]
            vt = jnp.concatenate([v_t, ones], axis=0)
            m_new = jnp.max(st, axis=0, keepdims=True)
            if not first:
                m_old = m_ref[ci]
                m_new = jnp.maximum(m_old, m_new)
            pv = jnp.dot(vt, jnp.exp2(st - m_new).astype(BF16), preferred_element_type=F32)
            acc_ref[ci] = pv if first else jnp.exp2(m_old - m_new) * acc_ref[ci] + pv
            m_ref[ci] = m_new
        return tuple(sts[N_CHAINS:])

    def chunks(i, a):
        for j in range(CHUNKS_PER_ITER):
            a = body(CHUNKS_PER_ITER * i + j, a)
        return a

    ahead = body(None, tuple(scores(None, ci) for ci in range(AHEAD)))
    lax.fori_loop(0, n_chunks // CHUNKS_PER_ITER, chunks, ahead)

    def normalized(ci):
        acc = acc_ref[ci]
        return acc[0:HD] * (1.0 / acc[HD:HD + 1])

    for kh in range(A_KV_HEADS):
        ot = normalized(kh)
        for g in range(group):
            h = kh * group + g
            o = ot[:, g * nq:(g + 1) * nq].T
            o_ref[0, :, h * HD:(h + 1) * HD] = (o * gate_ref[0, :, h * HD:(h + 1) * HD]).astype(BF16)

    lam = (jnp.exp(jnp.sum(lq1_ref[...] * lk1_ref[...], keepdims=True))
           - jnp.exp(jnp.sum(lq2_ref[...] * lk2_ref[...], keepdims=True)) + lam_init)
    for h in range(B_HEADS):
        c0 = (A_HEADS + h) * HD
        ot = normalized(A_KV_HEADS + h)
        od = ot[:, 0:nq] - lam * ot[:, nq:2 * nq]
        od = od * lax.rsqrt(jnp.mean(od * od, axis=0, keepdims=True) + EPS)
        o = od.T * subln_ref[...] * (1.0 - lam_init)
        o_ref[0, :, c0:c0 + HD] = (o * gate_ref[0, :, c0:c0 + HD]).astype(BF16)


def _attention(q, k_ctx, vt_ctx, k, vt, gate, lq1, lk1, lq2, lk2, subln_g, lam_init):
    assert CTX == TK
    const2 = lambda b, i: (0, 0)
    lamspec = pl.BlockSpec((1, B_HEAD_DIM), const2)
    qspec = pl.BlockSpec((1, TQ, D), lambda b, i: (b, i, 0))
    return pl.pallas_call(
        functools.partial(_attn_kernel, lam_init=lam_init),
        grid=(BATCH, SEQ // TQ),
        in_specs=[lamspec, lamspec, lamspec, lamspec,
                  pl.BlockSpec((1, HD), const2),
                  qspec,
                  pl.BlockSpec((1, CTX, K_COLS), lambda b, i: (b, 0, 0)),
                  pl.BlockSpec((1, V_ROWS, CTX), lambda b, i: (b, 0, 0)),
                  pl.BlockSpec((1, SEQ, K_COLS), lambda b, i: (b, 0, 0)),
                  pl.BlockSpec((1, SEQ // TK, V_ROWS, TK), lambda b, i: (b, 0, 0, 0)),
                  qspec],
        out_specs=qspec,
        out_shape=jax.ShapeDtypeStruct((BATCH, SEQ, D), BF16),
        scratch_shapes=[pltpu.VMEM((N_CHAINS, ACC_ROWS, 2 * TQ), F32),
                        pltpu.VMEM((N_CHAINS, 1, 2 * TQ), F32)],
        compiler_params=pltpu.CompilerParams(
            dimension_semantics=("arbitrary", "arbitrary"), vmem_limit_bytes=VMEM_LIMIT),
        name="attn",
    )(lq1.reshape(1, -1), lk1.reshape(1, -1), lq2.reshape(1, -1), lk2.reshape(1, -1),
      subln_g.reshape(1, HD), q, k_ctx, vt_ctx, k, vt, gate)


def _channel_dft():
    c = np.arange(F_GROUP_DIM)
    ang = 2.0 * np.pi * ((c[:, None] * c[None, :]) % F_GROUP_DIM) / F_GROUP_DIM
    return (np.concatenate([np.cos(ang), np.sin(ang)], axis=1) / 16.0).astype(np.float32)


_CHANNEL_DFT = _channel_dft()


def _mid_kernel(x_ref, og_ref, mod0_ref, mod1_ref, g_ref, wo_ref, wf_ref, cs_ref, x1_ref, ab_ref, sg_ref):
    x1 = x_ref[0] + mod0_ref[0, 2:3, :] * jnp.dot(og_ref[0], wo_ref[...], preferred_element_type=F32)
    x1_ref[0] = x1
    mod = mod1_ref[0]
    hb = (_rms_rows(x1) * (g_ref[...] * (1.0 + mod[1:2, :])) + mod[0:1, :]).astype(BF16)
    u = jnp.dot(hb, wf_ref[:, 0:D], preferred_element_type=F32).astype(BF16)
    sg_ref[0] = _silu(jnp.dot(hb, wf_ref[:, D:2 * D], preferred_element_type=F32)).astype(BF16)
    for gi in range(F_GROUPS):
        c0 = gi * F_GROUP_DIM
        ab = jnp.dot(u[:, c0:c0 + F_GROUP_DIM], cs_ref[...], preferred_element_type=F32)
        ab_ref[0, 0, :, c0:c0 + F_GROUP_DIM] = ab[:, 0:F_GROUP_DIM].astype(BF16)
        ab_ref[0, 1, :, c0:c0 + F_GROUP_DIM] = ab[:, F_GROUP_DIM:].astype(BF16)


def _mid(x, og, mod0, mod1, norm_g, w_out, w_f):
    const2 = lambda b, i: (0, 0)
    row = lambda b, i: (b, i, 0)
    modspec = pl.BlockSpec((1, 3, D), lambda b, i: (b, 0, 0))
    return pl.pallas_call(
        _mid_kernel,
        grid=(BATCH, SEQ // MID_ROWS),
        in_specs=[pl.BlockSpec((1, MID_ROWS, D), row), pl.BlockSpec((1, MID_ROWS, D), row), modspec, modspec,
                  pl.BlockSpec((1, D), const2), pl.BlockSpec((D, D), const2), pl.BlockSpec((D, 2 * D), const2),
                  pl.BlockSpec((F_GROUP_DIM, 2 * F_GROUP_DIM), const2)],
        out_specs=[pl.BlockSpec((1, MID_ROWS, D), row),
                   pl.BlockSpec((1, 2, MID_ROWS, D), lambda b, i: (b, 0, i, 0)),
                   pl.BlockSpec((1, MID_ROWS, D), row)],
        out_shape=[jax.ShapeDtypeStruct((BATCH, SEQ, D), F32),
                   jax.ShapeDtypeStruct((BATCH, 2, SEQ, D), BF16),
                   jax.ShapeDtypeStruct((BATCH, SEQ, D), BF16)],
        compiler_params=pltpu.CompilerParams(
            dimension_semantics=("arbitrary", "arbitrary"), vmem_limit_bytes=VMEM_LIMIT),
        name="mid",
    )(x, og, mod0, mod1, norm_g.reshape(1, D), w_out.astype(BF16), w_f.astype(BF16),
      jnp.asarray(_CHANNEL_DFT).astype(BF16))


RADIX = 16
FBLK = RADIX * RADIX
STAGE_BLOCKS = 4
STAGE_STEPS = RADIX // STAGE_BLOCKS


def _stage_matrices():
    w = lambda e: np.exp(-2j * np.pi * (e % SEQ) / SEQ)
    i = np.arange(RADIX)
    eye = np.eye(RADIX)
    g = np.einsum("cd,kac->dkac", eye, w(256 * i[:, None, None] * i[None, :, None] + i[:, None, None] * i[None, None, :]))
    g = g.reshape(FBLK, FBLK)
    m1 = np.block([[g.real, g.imag], [g.imag, -g.real]])
    h = np.einsum("jk,qbk->qjbk", eye, w(256 * i[:, None, None] * i[None, :, None] + 16 * i[None, None, :] * i[None, :, None]))
    h = h.reshape(FBLK, FBLK)
    m2 = np.block([[h.real, -h.imag], [h.imag, h.real]])
    ph = w(i[None, None, :] * (16 * i[:, None, None] + 256 * i[None, :, None]))
    e = np.einsum("jk,bqc->bqjck", eye, ph).reshape(RADIX, FBLK, FBLK)
    m3 = np.concatenate([e.real, -e.imag], axis=2) / 64.0
    return m1.astype(np.float32), m2.astype(np.float32), m3.astype(np.float32)


_M1, _M2, _M3 = _stage_matrices()


def _fourier_kernel(ab_ref, m1_ref, m2_ref, m3_ref, sg_ref, x1_ref, mod_ref, wo_ref, fg_ref, o_ref, z_ref):
    s = pl.program_id(1)

    @pl.when(s < STAGE_STEPS)
    def _():
        for i in range(STAGE_BLOCKS):
            b = s * STAGE_BLOCKS + i
            rhs = ab_ref[0, :, :, i].reshape(2 * FBLK, D)
            y = jnp.dot(m1_ref[...], rhs, preferred_element_type=F32).astype(BF16)
            z_ref[:, :, pl.ds(b, 1)] = y.reshape(2, RADIX, 1, RADIX, D)

    @pl.when((s >= STAGE_STEPS) & (s < 2 * STAGE_STEPS))
    def _():
        for i in range(STAGE_BLOCKS):
            c = (s - STAGE_STEPS) * STAGE_BLOCKS + i
            rhs = z_ref[:, pl.ds(c, 1)].reshape(2 * FBLK, D)
            y = jnp.dot(m2_ref[...], rhs, preferred_element_type=F32).astype(BF16)
            z_ref[:, pl.ds(c, 1)] = y.reshape(2, 1, RADIX, RADIX, D)

    @pl.when(s >= 2 * STAGE_STEPS)
    def _():
        def stage3(i):
            kb = (s - 2 * STAGE_STEPS) * STAGE_BLOCKS + i
            rhs = z_ref[:, :, pl.ds(kb, 1)].reshape(2 * FBLK, D)
            return jnp.dot(m3_ref[i], rhs, preferred_element_type=F32)

        f_next = stage3(0)
        for i in range(STAGE_BLOCKS):
            f = f_next
            if i + 1 < STAGE_BLOCKS:
                f_next = stage3(i + 1)
            fg = (f * sg_ref[0, :, i].reshape(FBLK, D).astype(F32)).astype(BF16)
            x2 = x1_ref[0, :, i].reshape(FBLK, D) + mod_ref[0, 2:3, :] * jnp.dot(fg, wo_ref[...], preferred_element_type=F32)
            o_ref[0, :, i] = (_rms_rows(x2) * fg_ref[...]).reshape(RADIX, RADIX, D)


def _fourier(ab, sg, x1, mod1, w_out, final_g):
    digits = (RADIX, RADIX, RADIX)
    const2 = lambda b, s: (0, 0)
    last = lambda s: jnp.clip(s - 2 * STAGE_STEPS, 0, STAGE_STEPS - 1)
    rows3 = pl.BlockSpec((1, RADIX, STAGE_BLOCKS, RADIX, D), lambda b, s: (b, 0, last(s), 0, 0))
    out = pl.pallas_call(
        _fourier_kernel,
        grid=(BATCH, 3 * STAGE_STEPS),
        in_specs=[pl.BlockSpec((1, 2, RADIX, STAGE_BLOCKS, RADIX, D),
                               lambda b, s: (b, 0, 0, jnp.minimum(s, STAGE_STEPS - 1), 0, 0)),
                  pl.BlockSpec((2 * FBLK, 2 * FBLK), const2),
                  pl.BlockSpec((2 * FBLK, 2 * FBLK), const2),
                  pl.BlockSpec((STAGE_BLOCKS, FBLK, 2 * FBLK), lambda b, s: (last(s), 0, 0)),
                  rows3, rows3,
                  pl.BlockSpec((1, 3, D), lambda b, s: (b, 0, 0)),
                  pl.BlockSpec((D, D), const2), pl.BlockSpec((1, D), const2)],
        out_specs=rows3,
        out_shape=jax.ShapeDtypeStruct((BATCH,) + digits + (D,), F32),
        scratch_shapes=[pltpu.VMEM((2,) + digits + (D,), BF16)],
        compiler_params=pltpu.CompilerParams(
            dimension_semantics=("arbitrary", "arbitrary"), vmem_limit_bytes=VMEM_LIMIT),
        name="fourier",
    )(ab.reshape((BATCH, 2) + digits + (D,)), jnp.asarray(_M1).astype(BF16), jnp.asarray(_M2).astype(BF16),
      jnp.asarray(_M3).astype(BF16), sg.reshape((BATCH,) + digits + (D,)), x1.reshape((BATCH,) + digits + (D,)),
      mod1, w_out.astype(BF16), final_g.reshape(1, D))
    return out.reshape(BATCH, SEQ, D)


def kernel(x, c, ctx, c_ctx, ada_w, ada_b, norm_g, attn_in_w, attn_qn_g, attn_kn_g, lam_q1, lam_k1, lam_q2, lam_k2,
           attn_subln_g, attn_out_w, fourier_in_w, fourier_out_w, final_g):
    assert x.shape == (BATCH, SEQ, D) and ctx.shape == (BATCH, CTX, D)
    assert ada_w.shape[0] == 2 and attn_in_w.shape == (1, D, IN_COLS)
    cond = jnp.concatenate([c, c_ctx[None, :], jnp.zeros((16 - BATCH - 1, D), F32)], axis=0)
    mods = _ada(cond, ada_w, ada_b)
    mod0 = mods[0].reshape(16, 3, D)
    mod1 = mods[1].reshape(16, 3, D)
    lam_init = 0.8 - 0.6 * math.exp(-0.3 * 0)

    k_ctx, vt_ctx, k, vt, q, gate = _inproj(x, ctx, mod0, norm_g[0], attn_in_w[0], attn_qn_g[0], attn_kn_g[0])
    og = _attention(q, k_ctx, vt_ctx, k, vt, gate, lam_q1[0], lam_k1[0], lam_q2[0], lam_k2[0], attn_subln_g[0], lam_init)
    x1, ab, sg = _mid(x, og, mod0, mod1, norm_g[1], attn_out_w[0], fourier_in_w[0])
    return _fourier(ab, sg, x1, mod1, fourier_out_w[0], final_g)
```

```python
import functools
import math

import numpy as np
import jax
import jax.numpy as jnp
from jax import lax
from jax.experimental import pallas as pl
from jax.experimental.pallas import tpu as pltpu

F32 = jnp.float32
BF16 = jnp.bfloat16

D = 1024
BATCH = 8
SEQ = 4096
CTX = 256
KEYS = CTX + SEQ
GRID_W = 64
ROPE_THETA = 10000.0
EPS = 1e-6
LOG2E = math.log2(math.e)

HD = 128
A_HEADS = 4
A_KV_HEADS = 2
B_HEADS = 4
B_HEAD_DIM = 64
KA0, VA0, KB0, VB0 = 0, 256, 512, 1024
K_COLS = 768
V_ROWS = 768
QA0, QB0, G0 = 1536, 2048, 2560
IN_COLS = 3584
F_GROUPS = 4
F_GROUP_DIM = 256

ROWS = 512
MID_ROWS = 512
TQ = 256
TK = 256
VMEM_LIMIT = 56 * 1024 * 1024


def _silu(v):
    return v * (1.0 / (1.0 + jnp.exp(-v)))


def _rms_rows(v):
    return v * lax.rsqrt(jnp.mean(v * v, axis=-1, keepdims=True) + EPS)


def _ada_kernel(c_ref, w_ref, b_ref, o_ref):
    s = _silu(c_ref[...]).astype(BF16)
    o_ref[0] = jnp.dot(s, w_ref[0].astype(BF16), preferred_element_type=F32) + b_ref[0]


def _ada(cond, ada_w, ada_b):
    depth, _, n3 = ada_w.shape
    tn = 768
    return pl.pallas_call(
        _ada_kernel,
        grid=(depth, n3 // tn),
        in_specs=[
            pl.BlockSpec((16, D), lambda l, j: (0, 0)),
            pl.BlockSpec((1, D, tn), lambda l, j: (l, 0, j)),
            pl.BlockSpec((1, 1, tn), lambda l, j: (l, 0, j)),
        ],
        out_specs=pl.BlockSpec((1, 16, tn), lambda l, j: (l, 0, j)),
        out_shape=jax.ShapeDtypeStruct((depth, 16, n3), F32),
        compiler_params=pltpu.CompilerParams(
            dimension_semantics=("arbitrary", "arbitrary"), vmem_limit_bytes=VMEM_LIMIT),
        name="ada",
    )(cond, ada_w, ada_b.reshape(depth, 1, n3))


def _rope_table(dim, maps):
    q4 = dim // 4
    pos = np.arange(SEQ)
    axis_pos = np.stack([pos // GRID_W, pos % GRID_W], axis=1).astype(np.float64)
    inv = ROPE_THETA ** (-np.arange(q4, dtype=np.float64) / q4)
    ang = axis_pos[:, None, None, :, None] * inv[None, None, None, None, :]
    ang = np.broadcast_to(ang, (SEQ, 2, maps, 2, q4))
    sign = np.array([-1.0, 1.0]).reshape(1, 2, 1, 1, 1)
    cos = np.cos(ang).reshape(SEQ, HD)
    sin = (np.sin(ang) * sign).reshape(SEQ, HD)
    return cos.astype(np.float32), sin.astype(np.float32)


_COS_A, _SIN_A = _rope_table(HD, 1)
_COS_B, _SIN_B = _rope_table(B_HEAD_DIM, 2)


def _rope_order(w, heads, maps):
    q4 = HD // maps // 4
    lead = w.shape[:-1]
    w = w.reshape(lead + (heads, maps, 2, 2, q4))
    n = len(lead)
    w = w.transpose(tuple(range(n)) + (n, n + 3, n + 1, n + 2, n + 4))
    return w.reshape(lead + (heads * HD,))


def _rope(y, cos, sin):
    return y * cos + pltpu.roll(y, HD // 2, 1) * sin


def _hidden(src, g_ref, mod):
    a = g_ref[...] * (1.0 + mod[1:2, :])
    return (_rms_rows(src) * a + mod[0:1, :]).astype(BF16)


def _keys_values(hb, w_ref, wvt_ref, kn_ref, k_ref, rope):
    ka = jnp.dot(hb, w_ref[:, KA0:KA0 + A_KV_HEADS * HD], preferred_element_type=F32)
    for h in range(A_KV_HEADS):
        y = _rms_rows(ka[:, h * HD:(h + 1) * HD]) * kn_ref[...]
        if rope is not None:
            y = _rope(y, rope[0], rope[1])
        k_ref[0, :, h * HD:(h + 1) * HD] = y.astype(BF16)
    kb = jnp.dot(hb, w_ref[:, KB0:KB0 + B_HEADS * HD], preferred_element_type=F32)
    for h in range(B_HEADS):
        y = kb[:, h * HD:(h + 1) * HD]
        if rope is not None:
            y = _rope(y, rope[2], rope[3])
        k_ref[0, :, (A_KV_HEADS + h) * HD:(A_KV_HEADS + h + 1) * HD] = y.astype(BF16)
    return lax.dot_general(wvt_ref[...], hb, (((1,), (1,)), ((), ())), preferred_element_type=F32).astype(BF16)


def _ctxproj_kernel(ctx_ref, mod_ref, g_ref, w_ref, wvt_ref, kn_ref, k_ref, vt_ref):
    hb = _hidden(ctx_ref[0], g_ref, mod_ref[0])
    vt_ref[0] = _keys_values(hb, w_ref, wvt_ref, kn_ref, k_ref, None)


def _inproj_kernel(x_ref, mod_ref, g_ref, w_ref, wvt_ref, qn_ref, kn_ref,
                   cosa_ref, sina_ref, cosb_ref, sinb_ref, k_ref, vt_ref, q_ref, gate_ref):
    hb = _hidden(x_ref[0], g_ref, mod_ref[0])
    rope = (cosa_ref[...], sina_ref[...], cosb_ref[...], sinb_ref[...])
    vt = _keys_values(hb, w_ref, wvt_ref, kn_ref, k_ref, rope)
    for j in range(ROWS // TK):
        vt_ref[0, j] = vt[:, j * TK:(j + 1) * TK]
    qa = jnp.dot(hb, w_ref[:, QA0:QA0 + A_HEADS * HD], preferred_element_type=F32)
    qgain = qn_ref[...] * (HD ** -0.5 * LOG2E)
    for h in range(A_HEADS):
        y = _rope(_rms_rows(qa[:, h * HD:(h + 1) * HD]) * qgain, rope[0], rope[1])
        q_ref[0, :, h * HD:(h + 1) * HD] = y.astype(BF16)
    qb = jnp.dot(hb, w_ref[:, QB0:QB0 + B_HEADS * HD], preferred_element_type=F32)
    for h in range(B_HEADS):
        y = _rope(qb[:, h * HD:(h + 1) * HD], rope[2], rope[3])
        q_ref[0, :, (A_HEADS + h) * HD:(A_HEADS + h + 1) * HD] = (y * (B_HEAD_DIM ** -0.5 * LOG2E)).astype(BF16)
    gate_ref[0] = _silu(jnp.dot(hb, w_ref[:, G0:G0 + D], preferred_element_type=F32)).astype(BF16)


def _inproj(x, ctx, mod, norm_g, w_in, qn_g, kn_g):
    const2 = lambda b, t: (0, 0)
    wb = jnp.concatenate([
        _rope_order(w_in[:, KA0:VA0], A_KV_HEADS, 1), w_in[:, VA0:KB0],
        _rope_order(w_in[:, KB0:VB0], B_HEADS, 2), w_in[:, VB0:QA0],
        _rope_order(w_in[:, QA0:QB0], A_HEADS, 1), _rope_order(w_in[:, QB0:G0], B_HEADS, 2),
        w_in[:, G0:]], axis=1).astype(BF16)
    wv = lax.optimization_barrier(jnp.concatenate([w_in[:, VA0:KB0], w_in[:, VB0:QA0]], axis=1))
    wvt = wv.T.astype(BF16)
    g2, kn2 = norm_g.reshape(1, D), _rope_order(kn_g, 1, 1).reshape(1, HD)
    params = pltpu.CompilerParams(dimension_semantics=("arbitrary", "arbitrary"), vmem_limit_bytes=VMEM_LIMIT)
    k_ctx, vt_ctx = pl.pallas_call(
        _ctxproj_kernel,
        grid=(BATCH, 1),
        in_specs=[pl.BlockSpec((1, CTX, D), lambda b, t: (b, 0, 0)),
                  pl.BlockSpec((1, 3, D), lambda b, t: (BATCH, 0, 0)),
                  pl.BlockSpec((1, D), const2), pl.BlockSpec((D, IN_COLS), const2),
                  pl.BlockSpec((V_ROWS, D), const2), pl.BlockSpec((1, HD), const2)],
        out_specs=[pl.BlockSpec((1, CTX, K_COLS), lambda b, t: (b, 0, 0)),
                   pl.BlockSpec((1, V_ROWS, CTX), lambda b, t: (b, 0, 0))],
        out_shape=[jax.ShapeDtypeStruct((BATCH, CTX, K_COLS), BF16),
                   jax.ShapeDtypeStruct((BATCH, V_ROWS, CTX), BF16)],
        compiler_params=params,
        name="ctxproj",
    )(ctx, mod, g2, wb, wvt, kn2)

    row = lambda b, t: (b, t, 0)
    table = pl.BlockSpec((ROWS, HD), lambda b, t: (t, 0))
    k, vt, q, gate = pl.pallas_call(
        _inproj_kernel,
        grid=(BATCH, SEQ // ROWS),
        in_specs=[pl.BlockSpec((1, ROWS, D), row),
                  pl.BlockSpec((1, 3, D), lambda b, t: (b, 0, 0)),
                  pl.BlockSpec((1, D), const2), pl.BlockSpec((D, IN_COLS), const2),
                  pl.BlockSpec((V_ROWS, D), const2), pl.BlockSpec((1, HD), const2), pl.BlockSpec((1, HD), const2),
                  table, table, table, table],
        out_specs=[pl.BlockSpec((1, ROWS, K_COLS), row),
                   pl.BlockSpec((1, ROWS // TK, V_ROWS, TK), lambda b, t: (b, t, 0, 0)),
                   pl.BlockSpec((1, ROWS, D), row),
                   pl.BlockSpec((1, ROWS, D), row)],
        out_shape=[jax.ShapeDtypeStruct((BATCH, SEQ, K_COLS), BF16),
                   jax.ShapeDtypeStruct((BATCH, SEQ // TK, V_ROWS, TK), BF16),
                   jax.ShapeDtypeStruct((BATCH, SEQ, D), BF16),
                   jax.ShapeDtypeStruct((BATCH, SEQ, D), BF16)],
        compiler_params=params,
        name="inproj",
    )(x, mod, g2, wb, wvt, _rope_order(qn_g, 1, 1).reshape(1, HD), kn2, _COS_A, _SIN_A, _COS_B, _SIN_B)
    return k_ctx, vt_ctx, k, vt, q, gate


N_CHAINS = A_KV_HEADS + B_HEADS
ACC_ROWS = HD + 16
AHEAD = 2
CHUNKS_PER_ITER = 8


def _attn_kernel(lq1_ref, lk1_ref, lq2_ref, lk2_ref, subln_ref, q_ref, kc_ref, vtc_ref, k_ref, vt_ref, gate_ref, o_ref,
                 acc_ref, m_ref, *, lam_init):
    nq = q_ref.shape[1]
    group = A_HEADS // A_KV_HEADS
    lane = lax.broadcasted_iota(jnp.int32, (nq, HD), 1)

    q2 = []
    for kh in range(A_KV_HEADS):
        q2.append(jnp.concatenate([q_ref[0, :, (kh * group + g) * HD:(kh * group + g + 1) * HD]
                                   for g in range(group)], axis=0))
    for h in range(B_HEADS):
        qh = q_ref[0, :, (A_HEADS + h) * HD:(A_HEADS + h + 1) * HD]
        zero = jnp.zeros_like(qh)
        map1 = (lane & (HD // 4)) == 0
        q2.append(jnp.concatenate([jnp.where(map1, qh, zero), jnp.where(map1, zero, qh)], axis=0))

    ones = jnp.concatenate([jnp.ones((1, TK), BF16), jnp.zeros((ACC_ROWS - HD - 1, TK), BF16)], axis=0)

    n_chunks = SEQ // TK

    def scores(c, ci):
        if c is None:
            k = kc_ref[0, :, ci * HD:(ci + 1) * HD]
        else:
            k = k_ref[0, pl.ds(pl.multiple_of(c * TK, TK), TK), ci * HD:(ci + 1) * HD]
        st = lax.dot_general(k, q2[ci], (((1,), (1,)), ((), ())), preferred_element_type=F32)
        return st.astype(BF16)

    def body(c, ahead):
        first = c is None
        c_next = 0 if first else jnp.minimum(c + 1, n_chunks - 1)
        sts = list(ahead)
        for ci in range(N_CHAINS):
            nxt = ci + AHEAD
            sts.append(scores(c, nxt) if nxt < N_CHAINS else scores(c_next, nxt - N_CHAINS))
            st = sts[ci]
            v_t = vtc_ref[0, ci * HD:(ci + 1) * HD, :] if first else vt_ref[0, c, ci * HD:(ci + 1) * HD, :]
            vt = jnp.concatenate([v_t, ones], axis=0)
            m_new = jnp.max(st, axis=0, keepdims=True).astype(F32)
            if not first:
                m_old = m_ref[ci]
                m_new = jnp.maximum(m_old, m_new)
            pv = jnp.dot(vt, jnp.exp2(st - m_new.astype(BF16)), preferred_element_type=F32)
            acc_ref[ci] = pv if first else jnp.exp2(m_old - m_new) * acc_ref[ci] + pv
            m_ref[ci] = m_new
        return tuple(sts[N_CHAINS:])

    def chunks(i, a):
        for j in range(CHUNKS_PER_ITER):
            a = body(CHUNKS_PER_ITER * i + j, a)
        return a

    ahead = body(None, tuple(scores(None, ci) for ci in range(AHEAD)))
    lax.fori_loop(0, n_chunks // CHUNKS_PER_ITER, chunks, ahead)

    def normalized(ci):
        acc = acc_ref[ci]
        return acc[0:HD] * (1.0 / acc[HD:HD + 1])

    for kh in range(A_KV_HEADS):
        ot = normalized(kh)
        for g in range(group):
            h = kh * group + g
            o = ot[:, g * nq:(g + 1) * nq].T
            o_ref[0, :, h * HD:(h + 1) * HD] = (o * gate_ref[0, :, h * HD:(h + 1) * HD]).astype(BF16)

    lam = (jnp.exp(jnp.sum(lq1_ref[...] * lk1_ref[...], keepdims=True))
           - jnp.exp(jnp.sum(lq2_ref[...] * lk2_ref[...], keepdims=True)) + lam_init)
    for h in range(B_HEADS):
        c0 = (A_HEADS + h) * HD
        ot = normalized(A_KV_HEADS + h)
        od = ot[:, 0:nq] - lam * ot[:, nq:2 * nq]
        od = od * lax.rsqrt(jnp.mean(od * od, axis=0, keepdims=True) + EPS)
        o = od.T * subln_ref[...] * (1.0 - lam_init)
        o_ref[0, :, c0:c0 + HD] = (o * gate_ref[0, :, c0:c0 + HD]).astype(BF16)


def _attention(q, k_ctx, vt_ctx, k, vt, gate, lq1, lk1, lq2, lk2, subln_g, lam_init):
    assert CTX == TK
    const2 = lambda b, i: (0, 0)
    lamspec = pl.BlockSpec((1, B_HEAD_DIM), const2)
    qspec = pl.BlockSpec((1, TQ, D), lambda b, i: (b, i, 0))
    return pl.pallas_call(
        functools.partial(_attn_kernel, lam_init=lam_init),
        grid=(BATCH, SEQ // TQ),
        in_specs=[lamspec, lamspec, lamspec, lamspec,
                  pl.BlockSpec((1, HD), const2),
                  qspec,
                  pl.BlockSpec((1, CTX, K_COLS), lambda b, i: (b, 0, 0)),
                  pl.BlockSpec((1, V_ROWS, CTX), lambda b, i: (b, 0, 0)),
                  pl.BlockSpec((1, SEQ, K_COLS), lambda b, i: (b, 0, 0)),
                  pl.BlockSpec((1, SEQ // TK, V_ROWS, TK), lambda b, i: (b, 0, 0, 0)),
                  qspec],
        out_specs=qspec,
        out_shape=jax.ShapeDtypeStruct((BATCH, SEQ, D), BF16),
        scratch_shapes=[pltpu.VMEM((N_CHAINS, ACC_ROWS, 2 * TQ), F32),
                        pltpu.VMEM((N_CHAINS, 1, 2 * TQ), F32)],
        compiler_params=pltpu.CompilerParams(
            dimension_semantics=("arbitrary", "arbitrary"), vmem_limit_bytes=VMEM_LIMIT),
        name="attn",
    )(lq1.reshape(1, -1), lk1.reshape(1, -1), lq2.reshape(1, -1), lk2.reshape(1, -1),
      subln_g.reshape(1, HD), q, k_ctx, vt_ctx, k, vt, gate)


def _channel_dft():
    c = np.arange(F_GROUP_DIM)
    ang = 2.0 * np.pi * ((c[:, None] * c[None, :]) % F_GROUP_DIM) / F_GROUP_DIM
    return (np.concatenate([np.cos(ang), np.sin(ang)], axis=1) / 16.0).astype(np.float32)


_CHANNEL_DFT = _channel_dft()


def _mid_kernel(x_ref, og_ref, mod0_ref, mod1_ref, g_ref, wo_ref, wf_ref, cs_ref, x1_ref, ab_ref, sg_ref):
    x1 = x_ref[0] + mod0_ref[0, 2:3, :] * jnp.dot(og_ref[0], wo_ref[...], preferred_element_type=F32)
    x1_ref[0] = x1
    mod = mod1_ref[0]
    hb = (_rms_rows(x1) * (g_ref[...] * (1.0 + mod[1:2, :])) + mod[0:1, :]).astype(BF16)
    u = jnp.dot(hb, wf_ref[:, 0:D], preferred_element_type=F32).astype(BF16)
    sg_ref[0] = _silu(jnp.dot(hb, wf_ref[:, D:2 * D], preferred_element_type=F32)).astype(BF16)
    for gi in range(F_GROUPS):
        c0 = gi * F_GROUP_DIM
        ab = jnp.dot(u[:, c0:c0 + F_GROUP_DIM], cs_ref[...], preferred_element_type=F32)
        ab_ref[0, 0, :, c0:c0 + F_GROUP_DIM] = ab[:, 0:F_GROUP_DIM].astype(BF16)
        ab_ref[0, 1, :, c0:c0 + F_GROUP_DIM] = ab[:, F_GROUP_DIM:].astype(BF16)


def _mid(x, og, mod0, mod1, norm_g, w_out, w_f):
    const2 = lambda b, i: (0, 0)
    row = lambda b, i: (b, i, 0)
    modspec = pl.BlockSpec((1, 3, D), lambda b, i: (b, 0, 0))
    return pl.pallas_call(
        _mid_kernel,
        grid=(BATCH, SEQ // MID_ROWS),
        in_specs=[pl.BlockSpec((1, MID_ROWS, D), row), pl.BlockSpec((1, MID_ROWS, D), row), modspec, modspec,
                  pl.BlockSpec((1, D), const2), pl.BlockSpec((D, D), const2), pl.BlockSpec((D, 2 * D), const2),
                  pl.BlockSpec((F_GROUP_DIM, 2 * F_GROUP_DIM), const2)],
        out_specs=[pl.BlockSpec((1, MID_ROWS, D), row),
                   pl.BlockSpec((1, 2, MID_ROWS, D), lambda b, i: (b, 0, i, 0)),
                   pl.BlockSpec((1, MID_ROWS, D), row)],
        out_shape=[jax.ShapeDtypeStruct((BATCH, SEQ, D), F32),
                   jax.ShapeDtypeStruct((BATCH, 2, SEQ, D), BF16),
                   jax.ShapeDtypeStruct((BATCH, SEQ, D), BF16)],
        compiler_params=pltpu.CompilerParams(
            dimension_semantics=("arbitrary", "arbitrary"), vmem_limit_bytes=VMEM_LIMIT),
        name="mid",
    )(x, og, mod0, mod1, norm_g.reshape(1, D), w_out.astype(BF16), w_f.astype(BF16),
      jnp.asarray(_CHANNEL_DFT).astype(BF16))


RADIX = 16
FBLK = RADIX * RADIX
STAGE_BLOCKS = 4
STAGE_STEPS = RADIX // STAGE_BLOCKS


def _stage_matrices():
    w = lambda e: np.exp(-2j * np.pi * (e % SEQ) / SEQ)
    i = np.arange(RADIX)
    eye = np.eye(RADIX)
    g = np.einsum("cd,kac->dkac", eye, w(256 * i[:, None, None] * i[None, :, None] + i[:, None, None] * i[None, None, :]))
    g = g.reshape(FBLK, FBLK)
    m1 = np.block([[g.real, g.imag], [g.imag, -g.real]])
    h = np.einsum("jk,qbk->qjbk", eye, w(256 * i[:, None, None] * i[None, :, None] + 16 * i[None, None, :] * i[None, :, None]))
    h = h.reshape(FBLK, FBLK)
    m2 = np.block([[h.real, -h.imag], [h.imag, h.real]])
    ph = w(i[None, None, :] * (16 * i[:, None, None] + 256 * i[None, :, None]))
    e = np.einsum("jk,bqc->bqjck", eye, ph).reshape(RADIX, FBLK, FBLK)
    m3 = np.concatenate([e.real, -e.imag], axis=2) / 64.0
    return m1.astype(np.float32), m2.astype(np.float32), m3.astype(np.float32)


_M1, _M2, _M3 = _stage_matrices()


def _fourier_kernel(ab_ref, m1_ref, m2_ref, m3_ref, sg_ref, x1_ref, mod_ref, wo_ref, fg_ref, o_ref, z_ref):
    s = pl.program_id(1)

    @pl.when(s < STAGE_STEPS)
    def _():
        for i in range(STAGE_BLOCKS):
            b = s * STAGE_BLOCKS + i
            rhs = ab_ref[0, :, :, i].reshape(2 * FBLK, D)
            y = jnp.dot(m1_ref[...], rhs, preferred_element_type=F32).astype(BF16)
            z_ref[:, :, pl.ds(b, 1)] = y.reshape(2, RADIX, 1, RADIX, D)

    @pl.when((s >= STAGE_STEPS) & (s < 2 * STAGE_STEPS))
    def _():
        for i in range(STAGE_BLOCKS):
            c = (s - STAGE_STEPS) * STAGE_BLOCKS + i
            rhs = z_ref[:, pl.ds(c, 1)].reshape(2 * FBLK, D)
            y = jnp.dot(m2_ref[...], rhs, preferred_element_type=F32).astype(BF16)
            z_ref[:, pl.ds(c, 1)] = y.reshape(2, 1, RADIX, RADIX, D)

    @pl.when(s >= 2 * STAGE_STEPS)
    def _():
        def stage3(i):
            kb = (s - 2 * STAGE_STEPS) * STAGE_BLOCKS + i
            rhs = z_ref[:, :, pl.ds(kb, 1)].reshape(2 * FBLK, D)
            return jnp.dot(m3_ref[i], rhs, preferred_element_type=F32)

        f_next = stage3(0)
        for i in range(STAGE_BLOCKS):
            f = f_next
            if i + 1 < STAGE_BLOCKS:
                f_next = stage3(i + 1)
            fg = (f * sg_ref[0, :, i].reshape(FBLK, D).astype(F32)).astype(BF16)
            x2 = x1_ref[0, :, i].reshape(FBLK, D) + mod_ref[0, 2:3, :] * jnp.dot(fg, wo_ref[...], preferred_element_type=F32)
            o_ref[0, :, i] = (_rms_rows(x2) * fg_ref[...]).reshape(RADIX, RADIX, D)


def _fourier(ab, sg, x1, mod1, w_out, final_g):
    digits = (RADIX, RADIX, RADIX)
    const2 = lambda b, s: (0, 0)
    last = lambda s: jnp.clip(s - 2 * STAGE_STEPS, 0, STAGE_STEPS - 1)
    rows3 = pl.BlockSpec((1, RADIX, STAGE_BLOCKS, RADIX, D), lambda b, s: (b, 0, last(s), 0, 0))
    out = pl.pallas_call(
        _fourier_kernel,
        grid=(BATCH, 3 * STAGE_STEPS),
        in_specs=[pl.BlockSpec((1, 2, RADIX, STAGE_BLOCKS, RADIX, D),
                               lambda b, s: (b, 0, 0, jnp.minimum(s, STAGE_STEPS - 1), 0, 0)),
                  pl.BlockSpec((2 * FBLK, 2 * FBLK), const2),
                  pl.BlockSpec((2 * FBLK, 2 * FBLK), const2),
                  pl.BlockSpec((STAGE_BLOCKS, FBLK, 2 * FBLK), lambda b, s: (last(s), 0, 0)),
                  rows3, rows3,
                  pl.BlockSpec((1, 3, D), lambda b, s: (b, 0, 0)),
                  pl.BlockSpec((D, D), const2), pl.BlockSpec((1, D), const2)],
        out_specs=rows3,
        out_shape=jax.ShapeDtypeStruct((BATCH,) + digits + (D,), F32),
        scratch_shapes=[pltpu.VMEM((2,) + digits + (D,), BF16)],
        compiler_params=pltpu.CompilerParams(
            dimension_semantics=("arbitrary", "arbitrary"), vmem_limit_bytes=VMEM_LIMIT),
        name="fourier",
    )(ab.reshape((BATCH, 2) + digits + (D,)), jnp.asarray(_M1).astype(BF16), jnp.asarray(_M2).astype(BF16),
      jnp.asarray(_M3).astype(BF16), sg.reshape((BATCH,) + digits + (D,)), x1.reshape((BATCH,) + digits + (D,)),
      mod1, w_out.astype(BF16), final_g.reshape(1, D))
    return out.reshape(BATCH, SEQ, D)


def kernel(x, c, ctx, c_ctx, ada_w, ada_b, norm_g, attn_in_w, attn_qn_g, attn_kn_g, lam_q1, lam_k1, lam_q2, lam_k2,
           attn_subln_g, attn_out_w, fourier_in_w, fourier_out_w, final_g):
    assert x.shape == (BATCH, SEQ, D) and ctx.shape == (BATCH, CTX, D)
    assert ada_w.shape[0] == 2 and attn_in_w.shape == (1, D, IN_COLS)
    cond = jnp.concatenate([c, c_ctx[None, :], jnp.zeros((16 - BATCH - 1, D), F32)], axis=0)
    mods = _ada(cond, ada_w, ada_b)
    mod0 = mods[0].reshape(16, 3, D)
    mod1 = mods[1].reshape(16, 3, D)
    lam_init = 0.8 - 0.6 * math.exp(-0.3 * 0)

    k_ctx, vt_ctx, k, vt, q, gate = _inproj(x, ctx, mod0, norm_g[0], attn_in_w[0], attn_qn_g[0], attn_kn_g[0])
    og = _attention(q, k_ctx, vt_ctx, k, vt, gate, lam_q1[0], lam_k1[0], lam_q2[0], lam_k2[0], attn_subln_g[0], lam_init)
    x1, ab, sg = _mid(x, og, mod0, mod1, norm_g[1], attn_out_w[0], fourier_in_w[0])
    return _fourier(ab, sg, x1, mod1, fourier_out_w[0], final_g)
```

```python
import functools
import math

import numpy as np
import jax
import jax.numpy as jnp
from jax import lax
from jax.experimental import pallas as pl
from jax.experimental.pallas import tpu as pltpu

F32 = jnp.float32
BF16 = jnp.bfloat16

D = 1024
BATCH = 8
SEQ = 4096
CTX = 256
KEYS = CTX + SEQ
GRID_W = 64
ROPE_THETA = 10000.0
EPS = 1e-6
LOG2E = math.log2(math.e)

HD = 128
A_HEADS = 4
A_KV_HEADS = 2
B_HEADS = 4
B_HEAD_DIM = 64
KA0, VA0, KB0, VB0 = 0, 256, 512, 1024
K_COLS = 768
V_ROWS = 768
QA0, QB0, G0 = 1536, 2048, 2560
IN_COLS = 3584
F_GROUPS = 4
F_GROUP_DIM = 256

ROWS = 512
MID_ROWS = 512
TQ = 256
TK = 256
VMEM_LIMIT = 56 * 1024 * 1024


def _silu(v):
    return v * (1.0 / (1.0 + jnp.exp(-v)))


def _rms_rows(v):
    return v * lax.rsqrt(jnp.mean(v * v, axis=-1, keepdims=True) + EPS)


def _ada_kernel(c_ref, w_ref, b_ref, o_ref):
    s = _silu(c_ref[...]).astype(BF16)
    o_ref[0] = jnp.dot(s, w_ref[0].astype(BF16), preferred_element_type=F32) + b_ref[0]


def _ada(cond, ada_w, ada_b):
    depth, _, n3 = ada_w.shape
    tn = 768
    return pl.pallas_call(
        _ada_kernel,
        grid=(depth, n3 // tn),
        in_specs=[
            pl.BlockSpec((16, D), lambda l, j: (0, 0)),
            pl.BlockSpec((1, D, tn), lambda l, j: (l, 0, j)),
            pl.BlockSpec((1, 1, tn), lambda l, j: (l, 0, j)),
        ],
        out_specs=pl.BlockSpec((1, 16, tn), lambda l, j: (l, 0, j)),
        out_shape=jax.ShapeDtypeStruct((depth, 16, n3), F32),
        compiler_params=pltpu.CompilerParams(
            dimension_semantics=("arbitrary", "arbitrary"), vmem_limit_bytes=VMEM_LIMIT),
        name="ada",
    )(cond, ada_w, ada_b.reshape(depth, 1, n3))


def _rope_table(dim, maps):
    q4 = dim // 4
    pos = np.arange(SEQ)
    axis_pos = np.stack([pos // GRID_W, pos % GRID_W], axis=1).astype(np.float64)
    inv = ROPE_THETA ** (-np.arange(q4, dtype=np.float64) / q4)
    ang = axis_pos[:, None, None, :, None] * inv[None, None, None, None, :]
    ang = np.broadcast_to(ang, (SEQ, 2, maps, 2, q4))
    sign = np.array([-1.0, 1.0]).reshape(1, 2, 1, 1, 1)
    cos = np.cos(ang).reshape(SEQ, HD)
    sin = (np.sin(ang) * sign).reshape(SEQ, HD)
    return cos.astype(np.float32), sin.astype(np.float32)


_COS_A, _SIN_A = _rope_table(HD, 1)
_COS_B, _SIN_B = _rope_table(B_HEAD_DIM, 2)


def _rope_order(w, heads, maps):
    q4 = HD // maps // 4
    lead = w.shape[:-1]
    w = w.reshape(lead + (heads, maps, 2, 2, q4))
    n = len(lead)
    w = w.transpose(tuple(range(n)) + (n, n + 3, n + 1, n + 2, n + 4))
    return w.reshape(lead + (heads * HD,))


def _rope(y, cos, sin):
    return y * cos + pltpu.roll(y, HD // 2, 1) * sin


def _hidden(src, g_ref, mod):
    a = g_ref[...] * (1.0 + mod[1:2, :])
    return (_rms_rows(src) * a + mod[0:1, :]).astype(BF16)


def _keys_values(hb, w_ref, wvt_ref, kn_ref, k_ref, rope):
    ka = jnp.dot(hb, w_ref[:, KA0:KA0 + A_KV_HEADS * HD], preferred_element_type=F32)
    for h in range(A_KV_HEADS):
        y = _rms_rows(ka[:, h * HD:(h + 1) * HD]) * kn_ref[...]
        if rope is not None:
            y = _rope(y, rope[0], rope[1])
        k_ref[0, :, h * HD:(h + 1) * HD] = y.astype(BF16)
    kb = jnp.dot(hb, w_ref[:, KB0:KB0 + B_HEADS * HD], preferred_element_type=F32)
    for h in range(B_HEADS):
        y = kb[:, h * HD:(h + 1) * HD]
        if rope is not None:
            y = _rope(y, rope[2], rope[3])
        k_ref[0, :, (A_KV_HEADS + h) * HD:(A_KV_HEADS + h + 1) * HD] = y.astype(BF16)
    return lax.dot_general(wvt_ref[...], hb, (((1,), (1,)), ((), ())), preferred_element_type=F32).astype(BF16)


def _ctxproj_kernel(ctx_ref, mod_ref, g_ref, w_ref, wvt_ref, kn_ref, k_ref, vt_ref):
    hb = _hidden(ctx_ref[0], g_ref, mod_ref[0])
    vt_ref[0] = _keys_values(hb, w_ref, wvt_ref, kn_ref, k_ref, None)


def _inproj_kernel(x_ref, mod_ref, g_ref, w_ref, wvt_ref, qn_ref, kn_ref,
                   cosa_ref, sina_ref, cosb_ref, sinb_ref, k_ref, vt_ref, q_ref, gate_ref):
    hb = _hidden(x_ref[0], g_ref, mod_ref[0])
    rope = (cosa_ref[...], sina_ref[...], cosb_ref[...], sinb_ref[...])
    vt = _keys_values(hb, w_ref, wvt_ref, kn_ref, k_ref, rope)
    for j in range(ROWS // TK):
        vt_ref[0, j] = vt[:, j * TK:(j + 1) * TK]
    qa = jnp.dot(hb, w_ref[:, QA0:QA0 + A_HEADS * HD], preferred_element_type=F32)
    qgain = qn_ref[...] * (HD ** -0.5 * LOG2E)
    for h in range(A_HEADS):
        y = _rope(_rms_rows(qa[:, h * HD:(h + 1) * HD]) * qgain, rope[0], rope[1])
        q_ref[0, :, h * HD:(h + 1) * HD] = y.astype(BF16)
    qb = jnp.dot(hb, w_ref[:, QB0:QB0 + B_HEADS * HD], preferred_element_type=F32)
    for h in range(B_HEADS):
        y = _rope(qb[:, h * HD:(h + 1) * HD], rope[2], rope[3])
        q_ref[0, :, (A_HEADS + h) * HD:(A_HEADS + h + 1) * HD] = (y * (B_HEAD_DIM ** -0.5 * LOG2E)).astype(BF16)
    gate_ref[0] = _silu(jnp.dot(hb, w_ref[:, G0:G0 + D], preferred_element_type=F32)).astype(BF16)


def _inproj(x, ctx, mod, norm_g, w_in, qn_g, kn_g):
    const2 = lambda b, t: (0, 0)
    wb = jnp.concatenate([
        _rope_order(w_in[:, KA0:VA0], A_KV_HEADS, 1), w_in[:, VA0:KB0],
        _rope_order(w_in[:, KB0:VB0], B_HEADS, 2), w_in[:, VB0:QA0],
        _rope_order(w_in[:, QA0:QB0], A_HEADS, 1), _rope_order(w_in[:, QB0:G0], B_HEADS, 2),
        w_in[:, G0:]], axis=1).astype(BF16)
    wvt = jnp.concatenate([w_in[:, VA0:VA0 + A_KV_HEADS * HD].T, w_in[:, VB0:VB0 + B_HEADS * HD].T],
                          axis=0).astype(BF16)
    g2, kn2 = norm_g.reshape(1, D), _rope_order(kn_g, 1, 1).reshape(1, HD)
    params = pltpu.CompilerParams(dimension_semantics=("arbitrary", "arbitrary"), vmem_limit_bytes=VMEM_LIMIT)
    k_ctx, vt_ctx = pl.pallas_call(
        _ctxproj_kernel,
        grid=(BATCH, 1),
        in_specs=[pl.BlockSpec((1, CTX, D), lambda b, t: (b, 0, 0)),
                  pl.BlockSpec((1, 3, D), lambda b, t: (BATCH, 0, 0)),
                  pl.BlockSpec((1, D), const2), pl.BlockSpec((D, IN_COLS), const2),
                  pl.BlockSpec((V_ROWS, D), const2), pl.BlockSpec((1, HD), const2)],
        out_specs=[pl.BlockSpec((1, CTX, K_COLS), lambda b, t: (b, 0, 0)),
                   pl.BlockSpec((1, V_ROWS, CTX), lambda b, t: (b, 0, 0))],
        out_shape=[jax.ShapeDtypeStruct((BATCH, CTX, K_COLS), BF16),
                   jax.ShapeDtypeStruct((BATCH, V_ROWS, CTX), BF16)],
        compiler_params=params,
        name="ctxproj",
    )(ctx, mod, g2, wb, wvt, kn2)

    row = lambda b, t: (b, t, 0)
    table = pl.BlockSpec((ROWS, HD), lambda b, t: (t, 0))
    k, vt, q, gate = pl.pallas_call(
        _inproj_kernel,
        grid=(BATCH, SEQ // ROWS),
        in_specs=[pl.BlockSpec((1, ROWS, D), row),
                  pl.BlockSpec((1, 3, D), lambda b, t: (b, 0, 0)),
                  pl.BlockSpec((1, D), const2), pl.BlockSpec((D, IN_COLS), const2),
                  pl.BlockSpec((V_ROWS, D), const2), pl.BlockSpec((1, HD), const2), pl.BlockSpec((1, HD), const2),
                  table, table, table, table],
        out_specs=[pl.BlockSpec((1, ROWS, K_COLS), row),
                   pl.BlockSpec((1, ROWS // TK, V_ROWS, TK), lambda b, t: (b, t, 0, 0)),
                   pl.BlockSpec((1, ROWS, D), row),
                   pl.BlockSpec((1, ROWS, D), row)],
        out_shape=[jax.ShapeDtypeStruct((BATCH, SEQ, K_COLS), BF16),
                   jax.ShapeDtypeStruct((BATCH, SEQ // TK, V_ROWS, TK), BF16),
                   jax.ShapeDtypeStruct((BATCH, SEQ, D), BF16),
                   jax.ShapeDtypeStruct((BATCH, SEQ, D), BF16)],
        compiler_params=params,
        name="inproj",
    )(x, mod, g2, wb, wvt, _rope_order(qn_g, 1, 1).reshape(1, HD), kn2, _COS_A, _SIN_A, _COS_B, _SIN_B)
    return k_ctx, vt_ctx, k, vt, q, gate


N_CHAINS = A_KV_HEADS + B_HEADS
ACC_ROWS = HD + 16
CHUNKS_PER_ITER = 8


def _attn_kernel(lq1_ref, lk1_ref, lq2_ref, lk2_ref, subln_ref, q_ref, kc_ref, vtc_ref, k_ref, vt_ref, gate_ref, o_ref,
                 acc_ref, m_ref, s_ref, *, lam_init):
    nq = q_ref.shape[1]
    group = A_HEADS // A_KV_HEADS
    lane = lax.broadcasted_iota(jnp.int32, (nq, HD), 1)

    q2 = []
    for kh in range(A_KV_HEADS):
        q2.append(jnp.concatenate([q_ref[0, :, (kh * group + g) * HD:(kh * group + g + 1) * HD]
                                   for g in range(group)], axis=0))
    for h in range(B_HEADS):
        qh = q_ref[0, :, (A_HEADS + h) * HD:(A_HEADS + h + 1) * HD]
        zero = jnp.zeros_like(qh)
        map1 = (lane & (HD // 4)) == 0
        q2.append(jnp.concatenate([jnp.where(map1, qh, zero), jnp.where(map1, zero, qh)], axis=0))

    ones = jnp.concatenate([jnp.ones((1, TK), BF16), jnp.zeros((ACC_ROWS - HD - 1, TK), BF16)], axis=0)
    n_chunks = SEQ // TK

    def scores(c, ci):
        if c is None:
            k = kc_ref[0, :, ci * HD:(ci + 1) * HD]
        else:
            k = k_ref[0, pl.ds(pl.multiple_of(c * TK, TK), TK), ci * HD:(ci + 1) * HD]
        return lax.dot_general(k, q2[ci], (((1,), (1,)), ((), ())), preferred_element_type=F32)

    def stage_scores(buf, c):
        for ci in range(N_CHAINS):
            s_ref[buf, ci] = scores(c, ci)

    def consume(buf, c):
        first = c is None
        for ci in range(N_CHAINS):
            st = s_ref[buf, ci]
            v_t = vtc_ref[0, ci * HD:(ci + 1) * HD, :] if first else vt_ref[0, c, ci * HD:(ci + 1) * HD, :]
            vt = jnp.concatenate([v_t, ones], axis=0)
            m_new = jnp.max(st, axis=0, keepdims=True)
            if not first:
                m_old = m_ref[ci]
                m_new = jnp.maximum(m_old, m_new)
            pv = jnp.dot(vt, jnp.exp2(st - m_new).astype(BF16), preferred_element_type=F32)
            acc_ref[ci] = pv if first else jnp.exp2(m_old - m_new) * acc_ref[ci] + pv
            m_ref[ci] = m_new

    def chunks(i, carry):
        for j in range(CHUNKS_PER_ITER):
            c = CHUNKS_PER_ITER * i + j
            stage_scores(j % 2, jnp.minimum(c + 1, n_chunks - 1))
            consume((j + 1) % 2, c)
        return carry

    stage_scores(0, None)
    stage_scores(1, 0)
    consume(0, None)
    lax.fori_loop(0, n_chunks // CHUNKS_PER_ITER, chunks, 0)

    def normalized(ci):
        acc = acc_ref[ci]
        return acc[0:HD] * (1.0 / acc[HD:HD + 1])

    for kh in range(A_KV_HEADS):
        ot = normalized(kh)
        for g in range(group):
            h = kh * group + g
            o = ot[:, g * nq:(g + 1) * nq].T
            o_ref[0, :, h * HD:(h + 1) * HD] = (o * gate_ref[0, :, h * HD:(h + 1) * HD]).astype(BF16)

    lam = (jnp.exp(jnp.sum(lq1_ref[...] * lk1_ref[...], keepdims=True))
           - jnp.exp(jnp.sum(lq2_ref[...] * lk2_ref[...], keepdims=True)) + lam_init)
    for h in range(B_HEADS):
        c0 = (A_HEADS + h) * HD
        ot = normalized(A_KV_HEADS + h)
        od = ot[:, 0:nq] - lam * ot[:, nq:2 * nq]
        od = od * lax.rsqrt(jnp.mean(od * od, axis=0, keepdims=True) + EPS)
        o = od.T * subln_ref[...] * (1.0 - lam_init)
        o_ref[0, :, c0:c0 + HD] = (o * gate_ref[0, :, c0:c0 + HD]).astype(BF16)


def _attention(q, k_ctx, vt_ctx, k, vt, gate, lq1, lk1, lq2, lk2, subln_g, lam_init):
    assert CTX == TK
    const2 = lambda b, i: (0, 0)
    lamspec = pl.BlockSpec((1, B_HEAD_DIM), const2)
    qspec = pl.BlockSpec((1, TQ, D), lambda b, i: (b, i, 0))
    return pl.pallas_call(
        functools.partial(_attn_kernel, lam_init=lam_init),
        grid=(BATCH, SEQ // TQ),
        in_specs=[lamspec, lamspec, lamspec, lamspec,
                  pl.BlockSpec((1, HD), const2),
                  qspec,
                  pl.BlockSpec((1, CTX, K_COLS), lambda b, i: (b, 0, 0)),
                  pl.BlockSpec((1, V_ROWS, CTX), lambda b, i: (b, 0, 0)),
                  pl.BlockSpec((1, SEQ, K_COLS), lambda b, i: (b, 0, 0)),
                  pl.BlockSpec((1, SEQ // TK, V_ROWS, TK), lambda b, i: (b, 0, 0, 0)),
                  qspec],
        out_specs=qspec,
        out_shape=jax.ShapeDtypeStruct((BATCH, SEQ, D), BF16),
        scratch_shapes=[pltpu.VMEM((N_CHAINS, ACC_ROWS, 2 * TQ), F32),
                        pltpu.VMEM((N_CHAINS, 1, 2 * TQ), F32),
                        pltpu.VMEM((2, N_CHAINS, TK, 2 * TQ), F32)],
        compiler_params=pltpu.CompilerParams(
            dimension_semantics=("arbitrary", "arbitrary"), vmem_limit_bytes=VMEM_LIMIT),
        name="attn",
    )(lq1.reshape(1, -1), lk1.reshape(1, -1), lq2.reshape(1, -1), lk2.reshape(1, -1),
      subln_g.reshape(1, HD), q, k_ctx, vt_ctx, k, vt, gate)


def _channel_dft():
    c = np.arange(F_GROUP_DIM)
    ang = 2.0 * np.pi * ((c[:, None] * c[None, :]) % F_GROUP_DIM) / F_GROUP_DIM
    return (np.concatenate([np.cos(ang), np.sin(ang)], axis=1) / 16.0).astype(np.float32)


_CHANNEL_DFT = _channel_dft()


def _mid_kernel(x_ref, og_ref, mod0_ref, mod1_ref, g_ref, wo_ref, wf_ref, cs_ref, x1_ref, ab_ref, sg_ref):
    x1 = x_ref[0] + mod0_ref[0, 2:3, :] * jnp.dot(og_ref[0], wo_ref[...], preferred_element_type=F32)
    x1_ref[0] = x1
    mod = mod1_ref[0]
    hb = (_rms_rows(x1) * (g_ref[...] * (1.0 + mod[1:2, :])) + mod[0:1, :]).astype(BF16)
    u = jnp.dot(hb, wf_ref[:, 0:D], preferred_element_type=F32).astype(BF16)
    sg_ref[0] = _silu(jnp.dot(hb, wf_ref[:, D:2 * D], preferred_element_type=F32)).astype(BF16)
    for gi in range(F_GROUPS):
        c0 = gi * F_GROUP_DIM
        ab = jnp.dot(u[:, c0:c0 + F_GROUP_DIM], cs_ref[...], preferred_element_type=F32)
        ab_ref[0, 0, :, c0:c0 + F_GROUP_DIM] = ab[:, 0:F_GROUP_DIM].astype(BF16)
        ab_ref[0, 1, :, c0:c0 + F_GROUP_DIM] = ab[:, F_GROUP_DIM:].astype(BF16)


def _mid(x, og, mod0, mod1, norm_g, w_out, w_f):
    const2 = lambda b, i: (0, 0)
    row = lambda b, i: (b, i, 0)
    modspec = pl.BlockSpec((1, 3, D), lambda b, i: (b, 0, 0))
    return pl.pallas_call(
        _mid_kernel,
        grid=(BATCH, SEQ // MID_ROWS),
        in_specs=[pl.BlockSpec((1, MID_ROWS, D), row), pl.BlockSpec((1, MID_ROWS, D), row), modspec, modspec,
                  pl.BlockSpec((1, D), const2), pl.BlockSpec((D, D), const2), pl.BlockSpec((D, 2 * D), const2),
                  pl.BlockSpec((F_GROUP_DIM, 2 * F_GROUP_DIM), const2)],
        out_specs=[pl.BlockSpec((1, MID_ROWS, D), row),
                   pl.BlockSpec((1, 2, MID_ROWS, D), lambda b, i: (b, 0, i, 0)),
                   pl.BlockSpec((1, MID_ROWS, D), row)],
        out_shape=[jax.ShapeDtypeStruct((BATCH, SEQ, D), F32),
                   jax.ShapeDtypeStruct((BATCH, 2, SEQ, D), BF16),
                   jax.ShapeDtypeStruct((BATCH, SEQ, D), BF16)],
        compiler_params=pltpu.CompilerParams(
            dimension_semantics=("arbitrary", "arbitrary"), vmem_limit_bytes=VMEM_LIMIT),
        name="mid",
    )(x, og, mod0, mod1, norm_g.reshape(1, D), w_out.astype(BF16), w_f.astype(BF16),
      jnp.asarray(_CHANNEL_DFT).astype(BF16))


RADIX = 16
FBLK = RADIX * RADIX
STAGE_BLOCKS = 4
STAGE_STEPS = RADIX // STAGE_BLOCKS


def _stage_matrices():
    w = lambda e: np.exp(-2j * np.pi * (e % SEQ) / SEQ)
    i = np.arange(RADIX)
    eye = np.eye(RADIX)
    g = np.einsum("cd,kac->dkac", eye, w(256 * i[:, None, None] * i[None, :, None] + i[:, None, None] * i[None, None, :]))
    g = g.reshape(FBLK, FBLK)
    m1 = np.block([[g.real, g.imag], [g.imag, -g.real]])
    h = np.einsum("jk,qbk->qjbk", eye, w(256 * i[:, None, None] * i[None, :, None] + 16 * i[None, None, :] * i[None, :, None]))
    h = h.reshape(FBLK, FBLK)
    m2 = np.block([[h.real, -h.imag], [h.imag, h.real]])
    ph = w(i[None, None, :] * (16 * i[:, None, None] + 256 * i[None, :, None]))
    e = np.einsum("jk,bqc->bqjck", eye, ph).reshape(RADIX, FBLK, FBLK)
    m3 = np.concatenate([e.real, -e.imag], axis=2) / 64.0
    return m1.astype(np.float32), m2.astype(np.float32), m3.astype(np.float32)


_M1, _M2, _M3 = _stage_matrices()


def _fourier_kernel(ab_ref, m1_ref, m2_ref, m3_ref, sg_ref, x1_ref, mod_ref, wo_ref, fg_ref, o_ref, z_ref):
    s = pl.program_id(1)

    @pl.when(s < STAGE_STEPS)
    def _():
        for i in range(STAGE_BLOCKS):
            b = s * STAGE_BLOCKS + i
            rhs = ab_ref[0, :, :, i].reshape(2 * FBLK, D)
            y = jnp.dot(m1_ref[...], rhs, preferred_element_type=F32).astype(BF16)
            z_ref[:, :, pl.ds(b, 1)] = y.reshape(2, RADIX, 1, RADIX, D)

    @pl.when((s >= STAGE_STEPS) & (s < 2 * STAGE_STEPS))
    def _():
        for i in range(STAGE_BLOCKS):
            c = (s - STAGE_STEPS) * STAGE_BLOCKS + i
            rhs = z_ref[:, pl.ds(c, 1)].reshape(2 * FBLK, D)
            y = jnp.dot(m2_ref[...], rhs, preferred_element_type=F32).astype(BF16)
            z_ref[:, pl.ds(c, 1)] = y.reshape(2, 1, RADIX, RADIX, D)

    @pl.when(s >= 2 * STAGE_STEPS)
    def _():
        def stage3(i):
            kb = (s - 2 * STAGE_STEPS) * STAGE_BLOCKS + i
            rhs = z_ref[:, :, pl.ds(kb, 1)].reshape(2 * FBLK, D)
            return jnp.dot(m3_ref[i], rhs, preferred_element_type=F32)

        f_next = stage3(0)
        for i in range(STAGE_BLOCKS):
            f = f_next
            if i + 1 < STAGE_BLOCKS:
                f_next = stage3(i + 1)
            fg = (f * sg_ref[0, :, i].reshape(FBLK, D).astype(F32)).astype(BF16)
            x2 = x1_ref[0, :, i].reshape(FBLK, D) + mod_ref[0, 2:3, :] * jnp.dot(fg, wo_ref[...], preferred_element_type=F32)
            o_ref[0, :, i] = (_rms_rows(x2) * fg_ref[...]).reshape(RADIX, RADIX, D)


def _fourier(ab, sg, x1, mod1, w_out, final_g):
    digits = (RADIX, RADIX, RADIX)
    const2 = lambda b, s: (0, 0)
    last = lambda s: jnp.clip(s - 2 * STAGE_STEPS, 0, STAGE_STEPS - 1)
    rows3 = pl.BlockSpec((1, RADIX, STAGE_BLOCKS, RADIX, D), lambda b, s: (b, 0, last(s), 0, 0))
    out = pl.pallas_call(
        _fourier_kernel,
        grid=(BATCH, 3 * STAGE_STEPS),
        in_specs=[pl.BlockSpec((1, 2, RADIX, STAGE_BLOCKS, RADIX, D),
                               lambda b, s: (b, 0, 0, jnp.minimum(s, STAGE_STEPS - 1), 0, 0)),
                  pl.BlockSpec((2 * FBLK, 2 * FBLK), const2),
                  pl.BlockSpec((2 * FBLK, 2 * FBLK), const2),
                  pl.BlockSpec((STAGE_BLOCKS, FBLK, 2 * FBLK), lambda b, s: (last(s), 0, 0)),
                  rows3, rows3,
                  pl.BlockSpec((1, 3, D), lambda b, s: (b, 0, 0)),
                  pl.BlockSpec((D, D), const2), pl.BlockSpec((1, D), const2)],
        out_specs=rows3,
        out_shape=jax.ShapeDtypeStruct((BATCH,) + digits + (D,), F32),
        scratch_shapes=[pltpu.VMEM((2,) + digits + (D,), BF16)],
        compiler_params=pltpu.CompilerParams(
            dimension_semantics=("arbitrary", "arbitrary"), vmem_limit_bytes=VMEM_LIMIT),
        name="fourier",
    )(ab.reshape((BATCH, 2) + digits + (D,)), jnp.asarray(_M1).astype(BF16), jnp.asarray(_M2).astype(BF16),
      jnp.asarray(_M3).astype(BF16), sg.reshape((BATCH,) + digits + (D,)), x1.reshape((BATCH,) + digits + (D,)),
      mod1, w_out.astype(BF16), final_g.reshape(1, D))
    return out.reshape(BATCH, SEQ, D)


def kernel(x, c, ctx, c_ctx, ada_w, ada_b, norm_g, attn_in_w, attn_qn_g, attn_kn_g, lam_q1, lam_k1, lam_q2, lam_k2,
           attn_subln_g, attn_out_w, fourier_in_w, fourier_out_w, final_g):
    assert x.shape == (BATCH, SEQ, D) and ctx.shape == (BATCH, CTX, D)
    assert ada_w.shape[0] == 2 and attn_in_w.shape == (1, D, IN_COLS)
    cond = jnp.concatenate([c, c_ctx[None, :], jnp.zeros((16 - BATCH - 1, D), F32)], axis=0)
    mods = _ada(cond, ada_w, ada_b)
    mod0 = mods[0].reshape(16, 3, D)
    mod1 = mods[1].reshape(16, 3, D)
    lam_init = 0.8 - 0.6 * math.exp(-0.3 * 0)

    k_ctx, vt_ctx, k, vt, q, gate = _inproj(x, ctx, mod0, norm_g[0], attn_in_w[0], attn_qn_g[0], attn_kn_g[0])
    og = _attention(q, k_ctx, vt_ctx, k, vt, gate, lam_q1[0], lam_k1[0], lam_q2[0], lam_k2[0], attn_subln_g[0], lam_init)
    x1, ab, sg = _mid(x, og, mod0, mod1, norm_g[1], attn_out_w[0], fourier_in_w[0])
    return _fourier(ab, sg, x1, mod1, fourier_out_w[0], final_g)
```

```python
import functools
import math

import numpy as np
import jax
import jax.numpy as jnp
from jax import lax
from jax.experimental import pallas as pl
from jax.experimental.pallas import tpu as pltpu

F32 = jnp.float32
BF16 = jnp.bfloat16

D = 1024
BATCH = 8
SEQ = 4096
CTX = 256
KEYS = CTX + SEQ
GRID_W = 64
ROPE_THETA = 10000.0
EPS = 1e-6
LOG2E = math.log2(math.e)

HD = 128
A_HEADS = 4
A_KV_HEADS = 2
B_HEADS = 4
B_HEAD_DIM = 64
KA0, VA0, KB0, VB0 = 0, 256, 512, 1024
K_COLS = 768
V_ROWS = 768
QA0, QB0, G0 = 1536, 2048, 2560
IN_COLS = 3584
F_GROUPS = 4
F_GROUP_DIM = 256

ROWS = 1024
MID_ROWS = 1024
TQ = 256
TK = 256
VMEM_LIMIT = 56 * 1024 * 1024


def _silu(v):
    return v * (1.0 / (1.0 + jnp.exp(-v)))


def _rms_rows(v):
    return v * lax.rsqrt(jnp.mean(v * v, axis=-1, keepdims=True) + EPS)


def _ada_kernel(c_ref, w_ref, b_ref, o_ref):
    s = _silu(c_ref[...]).astype(BF16)
    o_ref[0] = jnp.dot(s, w_ref[0].astype(BF16), preferred_element_type=F32) + b_ref[0]


def _ada(cond, ada_w, ada_b):
    depth, _, n3 = ada_w.shape
    tn = 768
    return pl.pallas_call(
        _ada_kernel,
        grid=(depth, n3 // tn),
        in_specs=[
            pl.BlockSpec((16, D), lambda l, j: (0, 0)),
            pl.BlockSpec((1, D, tn), lambda l, j: (l, 0, j)),
            pl.BlockSpec((1, 1, tn), lambda l, j: (l, 0, j)),
        ],
        out_specs=pl.BlockSpec((1, 16, tn), lambda l, j: (l, 0, j)),
        out_shape=jax.ShapeDtypeStruct((depth, 16, n3), F32),
        compiler_params=pltpu.CompilerParams(
            dimension_semantics=("arbitrary", "arbitrary"), vmem_limit_bytes=VMEM_LIMIT),
        name="ada",
    )(cond, ada_w, ada_b.reshape(depth, 1, n3))


def _rope_table(dim, maps):
    q4 = dim // 4
    pos = np.arange(SEQ)
    axis_pos = np.stack([pos // GRID_W, pos % GRID_W], axis=1).astype(np.float64)
    inv = ROPE_THETA ** (-np.arange(q4, dtype=np.float64) / q4)
    ang = axis_pos[:, None, None, :, None] * inv[None, None, None, None, :]
    ang = np.broadcast_to(ang, (SEQ, 2, maps, 2, q4))
    sign = np.array([-1.0, 1.0]).reshape(1, 2, 1, 1, 1)
    cos = np.cos(ang).reshape(SEQ, HD)
    sin = (np.sin(ang) * sign).reshape(SEQ, HD)
    return cos.astype(np.float32), sin.astype(np.float32)


_COS_A, _SIN_A = _rope_table(HD, 1)
_COS_B, _SIN_B = _rope_table(B_HEAD_DIM, 2)


def _rope_order(w, heads, maps):
    q4 = HD // maps // 4
    lead = w.shape[:-1]
    w = w.reshape(lead + (heads, maps, 2, 2, q4))
    n = len(lead)
    w = w.transpose(tuple(range(n)) + (n, n + 3, n + 1, n + 2, n + 4))
    return w.reshape(lead + (heads * HD,))


def _rope(y, cos, sin):
    return y * cos + pltpu.roll(y, HD // 2, 1) * sin


def _hidden(src, g_ref, mod):
    a = g_ref[...] * (1.0 + mod[1:2, :])
    return (_rms_rows(src) * a + mod[0:1, :]).astype(BF16)


def _keys_values(hb, w_ref, wvt_ref, kn_ref, k_ref, rope, rows=slice(None)):
    ka = jnp.dot(hb, w_ref[:, KA0:KA0 + A_KV_HEADS * HD], preferred_element_type=F32)
    for h in range(A_KV_HEADS):
        y = _rms_rows(ka[:, h * HD:(h + 1) * HD]) * kn_ref[...]
        if rope is not None:
            y = _rope(y, rope[0], rope[1])
        k_ref[0, rows, h * HD:(h + 1) * HD] = y.astype(BF16)
    kb = jnp.dot(hb, w_ref[:, KB0:KB0 + B_HEADS * HD], preferred_element_type=F32)
    for h in range(B_HEADS):
        y = kb[:, h * HD:(h + 1) * HD]
        if rope is not None:
            y = _rope(y, rope[2], rope[3])
        k_ref[0, rows, (A_KV_HEADS + h) * HD:(A_KV_HEADS + h + 1) * HD] = y.astype(BF16)
    return lax.dot_general(wvt_ref[...], hb, (((1,), (1,)), ((), ())), preferred_element_type=F32).astype(BF16)


def _ctxproj_kernel(ctx_ref, mod_ref, g_ref, w_ref, wvt_ref, kn_ref, k_ref, vt_ref):
    hb = _hidden(ctx_ref[0], g_ref, mod_ref[0])
    vt_ref[0] = _keys_values(hb, w_ref, wvt_ref, kn_ref, k_ref, None)


def _inproj_kernel(x_ref, mod_ref, g_ref, w_ref, wvt_ref, qn_ref, kn_ref,
                   cosa_ref, sina_ref, cosb_ref, sinb_ref, k_ref, vt_ref, q_ref, gate_ref):
    pieces = [slice(j * TK, (j + 1) * TK) for j in range(ROWS // TK)]
    hbs = [_hidden(x_ref[0, r], g_ref, mod_ref[0]) for r in pieces]
    qgain = qn_ref[...] * (HD ** -0.5 * LOG2E)
    for j, (r, hb) in enumerate(zip(pieces, hbs)):
        rope = (cosa_ref[r, :], sina_ref[r, :], cosb_ref[r, :], sinb_ref[r, :])
        vt_ref[0, j] = _keys_values(hb, w_ref, wvt_ref, kn_ref, k_ref, rope, r)
        qa = jnp.dot(hb, w_ref[:, QA0:QA0 + A_HEADS * HD], preferred_element_type=F32)
        for h in range(A_HEADS):
            y = _rope(_rms_rows(qa[:, h * HD:(h + 1) * HD]) * qgain, rope[0], rope[1])
            q_ref[0, r, h * HD:(h + 1) * HD] = y.astype(BF16)
        qb = jnp.dot(hb, w_ref[:, QB0:QB0 + B_HEADS * HD], preferred_element_type=F32)
        for h in range(B_HEADS):
            y = _rope(qb[:, h * HD:(h + 1) * HD], rope[2], rope[3])
            q_ref[0, r, (A_HEADS + h) * HD:(A_HEADS + h + 1) * HD] = (y * (B_HEAD_DIM ** -0.5 * LOG2E)).astype(BF16)
        gate_ref[0, r] = _silu(jnp.dot(hb, w_ref[:, G0:G0 + D], preferred_element_type=F32)).astype(BF16)


def _inproj(x, ctx, mod, norm_g, w_in, qn_g, kn_g):
    const2 = lambda b, t: (0, 0)
    wb = jnp.concatenate([
        _rope_order(w_in[:, KA0:VA0], A_KV_HEADS, 1), w_in[:, VA0:KB0],
        _rope_order(w_in[:, KB0:VB0], B_HEADS, 2), w_in[:, VB0:QA0],
        _rope_order(w_in[:, QA0:QB0], A_HEADS, 1), _rope_order(w_in[:, QB0:G0], B_HEADS, 2),
        w_in[:, G0:]], axis=1).astype(BF16)
    wvt = jnp.concatenate([w_in[:, VA0:VA0 + A_KV_HEADS * HD].T, w_in[:, VB0:VB0 + B_HEADS * HD].T],
                          axis=0).astype(BF16)
    g2, kn2 = norm_g.reshape(1, D), _rope_order(kn_g, 1, 1).reshape(1, HD)
    params = pltpu.CompilerParams(dimension_semantics=("arbitrary", "arbitrary"), vmem_limit_bytes=VMEM_LIMIT)
    k_ctx, vt_ctx = pl.pallas_call(
        _ctxproj_kernel,
        grid=(BATCH, 1),
        in_specs=[pl.BlockSpec((1, CTX, D), lambda b, t: (b, 0, 0)),
                  pl.BlockSpec((1, 3, D), lambda b, t: (BATCH, 0, 0)),
                  pl.BlockSpec((1, D), const2), pl.BlockSpec((D, IN_COLS), const2),
                  pl.BlockSpec((V_ROWS, D), const2), pl.BlockSpec((1, HD), const2)],
        out_specs=[pl.BlockSpec((1, CTX, K_COLS), lambda b, t: (b, 0, 0)),
                   pl.BlockSpec((1, V_ROWS, CTX), lambda b, t: (b, 0, 0))],
        out_shape=[jax.ShapeDtypeStruct((BATCH, CTX, K_COLS), BF16),
                   jax.ShapeDtypeStruct((BATCH, V_ROWS, CTX), BF16)],
        compiler_params=params,
        name="ctxproj",
    )(ctx, mod, g2, wb, wvt, kn2)

    row = lambda b, t: (b, t, 0)
    table = pl.BlockSpec((ROWS, HD), lambda b, t: (t, 0))
    k, vt, q, gate = pl.pallas_call(
        _inproj_kernel,
        grid=(BATCH, SEQ // ROWS),
        in_specs=[pl.BlockSpec((1, ROWS, D), row),
                  pl.BlockSpec((1, 3, D), lambda b, t: (b, 0, 0)),
                  pl.BlockSpec((1, D), const2), pl.BlockSpec((D, IN_COLS), const2),
                  pl.BlockSpec((V_ROWS, D), const2), pl.BlockSpec((1, HD), const2), pl.BlockSpec((1, HD), const2),
                  table, table, table, table],
        out_specs=[pl.BlockSpec((1, ROWS, K_COLS), row),
                   pl.BlockSpec((1, ROWS // TK, V_ROWS, TK), lambda b, t: (b, t, 0, 0)),
                   pl.BlockSpec((1, ROWS, D), row),
                   pl.BlockSpec((1, ROWS, D), row)],
        out_shape=[jax.ShapeDtypeStruct((BATCH, SEQ, K_COLS), BF16),
                   jax.ShapeDtypeStruct((BATCH, SEQ // TK, V_ROWS, TK), BF16),
                   jax.ShapeDtypeStruct((BATCH, SEQ, D), BF16),
                   jax.ShapeDtypeStruct((BATCH, SEQ, D), BF16)],
        compiler_params=params,
        name="inproj",
    )(x, mod, g2, wb, wvt, _rope_order(qn_g, 1, 1).reshape(1, HD), kn2, _COS_A, _SIN_A, _COS_B, _SIN_B)
    return k_ctx, vt_ctx, k, vt, q, gate


N_CHAINS = A_KV_HEADS + B_HEADS
ACC_ROWS = HD + 16
CHUNKS_PER_ITER = 8


def _attn_kernel(lq1_ref, lk1_ref, lq2_ref, lk2_ref, subln_ref, q_ref, kc_ref, vtc_ref, k_ref, vt_ref, gate_ref, o_ref,
                 acc_ref, m_ref, s_ref, *, lam_init):
    nq = q_ref.shape[1]
    group = A_HEADS // A_KV_HEADS
    lane = lax.broadcasted_iota(jnp.int32, (nq, HD), 1)

    q2 = []
    for kh in range(A_KV_HEADS):
        q2.append(jnp.concatenate([q_ref[0, :, (kh * group + g) * HD:(kh * group + g + 1) * HD]
                                   for g in range(group)], axis=0))
    for h in range(B_HEADS):
        qh = q_ref[0, :, (A_HEADS + h) * HD:(A_HEADS + h + 1) * HD]
        zero = jnp.zeros_like(qh)
        map1 = (lane & (HD // 4)) == 0
        q2.append(jnp.concatenate([jnp.where(map1, qh, zero), jnp.where(map1, zero, qh)], axis=0))

    ones = jnp.concatenate([jnp.ones((1, TK), BF16), jnp.zeros((ACC_ROWS - HD - 1, TK), BF16)], axis=0)
    n_chunks = SEQ // TK

    def scores(c, ci):
        if c is None:
            k = kc_ref[0, :, ci * HD:(ci + 1) * HD]
        else:
            k = k_ref[0, pl.ds(pl.multiple_of(c * TK, TK), TK), ci * HD:(ci + 1) * HD]
        return lax.dot_general(k, q2[ci], (((1,), (1,)), ((), ())), preferred_element_type=F32)

    def stage_scores(buf, c):
        for ci in range(N_CHAINS):
            s_ref[buf, ci] = scores(c, ci)

    def consume(buf, c):
        first = c is None
        for ci in range(N_CHAINS):
            st = s_ref[buf, ci]
            v_t = vtc_ref[0, ci * HD:(ci + 1) * HD, :] if first else vt_ref[0, c, ci * HD:(ci + 1) * HD, :]
            vt = jnp.concatenate([v_t, ones], axis=0)
            m_new = jnp.max(st, axis=0, keepdims=True)
            if not first:
                m_old = m_ref[ci]
                m_new = jnp.maximum(m_old, m_new)
            pv = jnp.dot(vt, jnp.exp2(st - m_new).astype(BF16), preferred_element_type=F32)
            acc_ref[ci] = pv if first else jnp.exp2(m_old - m_new) * acc_ref[ci] + pv
            m_ref[ci] = m_new

    def chunks(i, carry):
        for j in range(CHUNKS_PER_ITER):
            c = CHUNKS_PER_ITER * i + j
            stage_scores(j % 2, jnp.minimum(c + 1, n_chunks - 1))
            consume((j + 1) % 2, c)
        return carry

    stage_scores(0, None)
    stage_scores(1, 0)
    consume(0, None)
    lax.fori_loop(0, n_chunks // CHUNKS_PER_ITER, chunks, 0)

    def normalized(ci):
        acc = acc_ref[ci]
        return acc[0:HD] * (1.0 / acc[HD:HD + 1])

    for kh in range(A_KV_HEADS):
        ot = normalized(kh)
        for g in range(group):
            h = kh * group + g
            o = ot[:, g * nq:(g + 1) * nq].T
            o_ref[0, :, h * HD:(h + 1) * HD] = (o * gate_ref[0, :, h * HD:(h + 1) * HD]).astype(BF16)

    lam = (jnp.exp(jnp.sum(lq1_ref[...] * lk1_ref[...], keepdims=True))
           - jnp.exp(jnp.sum(lq2_ref[...] * lk2_ref[...], keepdims=True)) + lam_init)
    for h in range(B_HEADS):
        c0 = (A_HEADS + h) * HD
        ot = normalized(A_KV_HEADS + h)
        od = ot[:, 0:nq] - lam * ot[:, nq:2 * nq]
        od = od * lax.rsqrt(jnp.mean(od * od, axis=0, keepdims=True) + EPS)
        o = od.T * subln_ref[...] * (1.0 - lam_init)
        o_ref[0, :, c0:c0 + HD] = (o * gate_ref[0, :, c0:c0 + HD]).astype(BF16)


def _attention(q, k_ctx, vt_ctx, k, vt, gate, lq1, lk1, lq2, lk2, subln_g, lam_init):
    assert CTX == TK
    const2 = lambda b, i: (0, 0)
    lamspec = pl.BlockSpec((1, B_HEAD_DIM), const2)
    qspec = pl.BlockSpec((1, TQ, D), lambda b, i: (b, i, 0))
    return pl.pallas_call(
        functools.partial(_attn_kernel, lam_init=lam_init),
        grid=(BATCH, SEQ // TQ),
        in_specs=[lamspec, lamspec, lamspec, lamspec,
                  pl.BlockSpec((1, HD), const2),
                  qspec,
                  pl.BlockSpec((1, CTX, K_COLS), lambda b, i: (b, 0, 0)),
                  pl.BlockSpec((1, V_ROWS, CTX), lambda b, i: (b, 0, 0)),
                  pl.BlockSpec((1, SEQ, K_COLS), lambda b, i: (b, 0, 0)),
                  pl.BlockSpec((1, SEQ // TK, V_ROWS, TK), lambda b, i: (b, 0, 0, 0)),
                  qspec],
        out_specs=qspec,
        out_shape=jax.ShapeDtypeStruct((BATCH, SEQ, D), BF16),
        scratch_shapes=[pltpu.VMEM((N_CHAINS, ACC_ROWS, 2 * TQ), F32),
                        pltpu.VMEM((N_CHAINS, 1, 2 * TQ), F32),
                        pltpu.VMEM((2, N_CHAINS, TK, 2 * TQ), F32)],
        compiler_params=pltpu.CompilerParams(
            dimension_semantics=("arbitrary", "arbitrary"), vmem_limit_bytes=VMEM_LIMIT),
        name="attn",
    )(lq1.reshape(1, -1), lk1.reshape(1, -1), lq2.reshape(1, -1), lk2.reshape(1, -1),
      subln_g.reshape(1, HD), q, k_ctx, vt_ctx, k, vt, gate)


def _channel_dft():
    c = np.arange(F_GROUP_DIM)
    ang = 2.0 * np.pi * ((c[:, None] * c[None, :]) % F_GROUP_DIM) / F_GROUP_DIM
    return (np.concatenate([np.cos(ang), np.sin(ang)], axis=1) / 16.0).astype(np.float32)


_CHANNEL_DFT = _channel_dft()


def _mid_kernel(x_ref, og_ref, mod0_ref, mod1_ref, g_ref, wo_ref, wf_ref, cs_ref, x1_ref, ab_ref, sg_ref):
    mod = mod1_ref[0]
    pieces = [slice(h * TK, (h + 1) * TK) for h in range(MID_ROWS // TK)]

    def out_proj(r):
        return jnp.dot(og_ref[0, r], wo_ref[...], preferred_element_type=F32)

    def channel_dft(r, u):
        for gi in range(F_GROUPS):
            c0 = gi * F_GROUP_DIM
            ab = jnp.dot(u[:, c0:c0 + F_GROUP_DIM], cs_ref[...], preferred_element_type=F32)
            ab_ref[0, 0, r, c0:c0 + F_GROUP_DIM] = ab[:, 0:F_GROUP_DIM].astype(BF16)
            ab_ref[0, 1, r, c0:c0 + F_GROUP_DIM] = ab[:, F_GROUP_DIM:].astype(BF16)

    a_next, pending = out_proj(pieces[0]), None
    for p, r in enumerate(pieces):
        a = a_next
        if p + 1 < len(pieces):
            a_next = out_proj(pieces[p + 1])
        x1 = x_ref[0, r] + mod0_ref[0, 2:3, :] * a
        x1_ref[0, r] = x1
        hb = (_rms_rows(x1) * (g_ref[...] * (1.0 + mod[1:2, :])) + mod[0:1, :]).astype(BF16)
        u = jnp.dot(hb, wf_ref[:, 0:D], preferred_element_type=F32).astype(BF16)
        sg_ref[0, r] = _silu(jnp.dot(hb, wf_ref[:, D:2 * D], preferred_element_type=F32)).astype(BF16)
        if pending is not None:
            channel_dft(*pending)
        pending = (r, u)
    channel_dft(*pending)


def _mid(x, og, mod0, mod1, norm_g, w_out, w_f):
    const2 = lambda b, i: (0, 0)
    row = lambda b, i: (b, i, 0)
    modspec = pl.BlockSpec((1, 3, D), lambda b, i: (b, 0, 0))
    return pl.pallas_call(
        _mid_kernel,
        grid=(BATCH, SEQ // MID_ROWS),
        in_specs=[pl.BlockSpec((1, MID_ROWS, D), row), pl.BlockSpec((1, MID_ROWS, D), row), modspec, modspec,
                  pl.BlockSpec((1, D), const2), pl.BlockSpec((D, D), const2), pl.BlockSpec((D, 2 * D), const2),
                  pl.BlockSpec((F_GROUP_DIM, 2 * F_GROUP_DIM), const2)],
        out_specs=[pl.BlockSpec((1, MID_ROWS, D), row),
                   pl.BlockSpec((1, 2, MID_ROWS, D), lambda b, i: (b, 0, i, 0)),
                   pl.BlockSpec((1, MID_ROWS, D), row)],
        out_shape=[jax.ShapeDtypeStruct((BATCH, SEQ, D), F32),
                   jax.ShapeDtypeStruct((BATCH, 2, SEQ, D), BF16),
                   jax.ShapeDtypeStruct((BATCH, SEQ, D), BF16)],
        compiler_params=pltpu.CompilerParams(
            dimension_semantics=("arbitrary", "arbitrary"), vmem_limit_bytes=VMEM_LIMIT),
        name="mid",
    )(x, og, mod0, mod1, norm_g.reshape(1, D), w_out.astype(BF16), w_f.astype(BF16),
      jnp.asarray(_CHANNEL_DFT).astype(BF16))


RADIX = 16
FBLK = RADIX * RADIX
STAGE_BLOCKS = 4
STAGE_STEPS = RADIX // STAGE_BLOCKS


def _stage_matrices():
    w = lambda e: np.exp(-2j * np.pi * (e % SEQ) / SEQ)
    i = np.arange(RADIX)
    eye = np.eye(RADIX)
    g = np.einsum("cd,kac->dkac", eye, w(256 * i[:, None, None] * i[None, :, None] + i[:, None, None] * i[None, None, :]))
    g = g.reshape(FBLK, FBLK)
    m1 = np.block([[g.real, g.imag], [g.imag, -g.real]])
    h = np.einsum("jk,qbk->qjbk", eye, w(256 * i[:, None, None] * i[None, :, None] + 16 * i[None, None, :] * i[None, :, None]))
    h = h.reshape(FBLK, FBLK)
    m2 = np.block([[h.real, -h.imag], [h.imag, h.real]])
    ph = w(i[None, None, :] * (16 * i[:, None, None] + 256 * i[None, :, None]))
    e = np.einsum("jk,bqc->bqjck", eye, ph).reshape(RADIX, FBLK, FBLK)
    m3 = np.concatenate([e.real, -e.imag], axis=2) / 64.0
    return m1.astype(np.float32), m2.astype(np.float32), m3.astype(np.float32)


_M1, _M2, _M3 = _stage_matrices()


def _fourier_kernel(ab_ref, m1_ref, m2_ref, m3_ref, sg_ref, x1_ref, mod_ref, wo_ref, fg_ref, o_ref, z_ref):
    s = pl.program_id(1)

    @pl.when(s < STAGE_STEPS)
    def _():
        for i in range(STAGE_BLOCKS):
            b = s * STAGE_BLOCKS + i
            rhs = ab_ref[0, :, :, i].reshape(2 * FBLK, D)
            y = jnp.dot(m1_ref[...], rhs, preferred_element_type=F32).astype(BF16)
            z_ref[:, :, pl.ds(b, 1)] = y.reshape(2, RADIX, 1, RADIX, D)

    @pl.when((s >= STAGE_STEPS) & (s < 2 * STAGE_STEPS))
    def _():
        for i in range(STAGE_BLOCKS):
            c = (s - STAGE_STEPS) * STAGE_BLOCKS + i
            rhs = z_ref[:, pl.ds(c, 1)].reshape(2 * FBLK, D)
            y = jnp.dot(m2_ref[...], rhs, preferred_element_type=F32).astype(BF16)
            z_ref[:, pl.ds(c, 1)] = y.reshape(2, 1, RADIX, RADIX, D)

    @pl.when(s >= 2 * STAGE_STEPS)
    def _():
        def stage3(i):
            kb = (s - 2 * STAGE_STEPS) * STAGE_BLOCKS + i
            rhs = z_ref[:, :, pl.ds(kb, 1)].reshape(2 * FBLK, D)
            return jnp.dot(m3_ref[i], rhs, preferred_element_type=F32)

        f_next = stage3(0)
        for i in range(STAGE_BLOCKS):
            f = f_next
            if i + 1 < STAGE_BLOCKS:
                f_next = stage3(i + 1)
            fg = (f * sg_ref[0, :, i].reshape(FBLK, D).astype(F32)).astype(BF16)
            x2 = x1_ref[0, :, i].reshape(FBLK, D) + mod_ref[0, 2:3, :] * jnp.dot(fg, wo_ref[...], preferred_element_type=F32)
            o_ref[0, :, i] = (_rms_rows(x2) * fg_ref[...]).reshape(RADIX, RADIX, D)


def _fourier(ab, sg, x1, mod1, w_out, final_g):
    digits = (RADIX, RADIX, RADIX)
    const2 = lambda b, s: (0, 0)
    last = lambda s: jnp.clip(s - 2 * STAGE_STEPS, 0, STAGE_STEPS - 1)
    rows3 = pl.BlockSpec((1, RADIX, STAGE_BLOCKS, RADIX, D), lambda b, s: (b, 0, last(s), 0, 0))
    out = pl.pallas_call(
        _fourier_kernel,
        grid=(BATCH, 3 * STAGE_STEPS),
        in_specs=[pl.BlockSpec((1, 2, RADIX, STAGE_BLOCKS, RADIX, D),
                               lambda b, s: (b, 0, 0, jnp.minimum(s, STAGE_STEPS - 1), 0, 0)),
                  pl.BlockSpec((2 * FBLK, 2 * FBLK), const2),
                  pl.BlockSpec((2 * FBLK, 2 * FBLK), const2),
                  pl.BlockSpec((STAGE_BLOCKS, FBLK, 2 * FBLK), lambda b, s: (last(s), 0, 0)),
                  rows3, rows3,
                  pl.BlockSpec((1, 3, D), lambda b, s: (b, 0, 0)),
                  pl.BlockSpec((D, D), const2), pl.BlockSpec((1, D), const2)],
        out_specs=rows3,
        out_shape=jax.ShapeDtypeStruct((BATCH,) + digits + (D,), F32),
        scratch_shapes=[pltpu.VMEM((2,) + digits + (D,), BF16)],
        compiler_params=pltpu.CompilerParams(
            dimension_semantics=("arbitrary", "arbitrary"), vmem_limit_bytes=VMEM_LIMIT),
        name="fourier",
    )(ab.reshape((BATCH, 2) + digits + (D,)), jnp.asarray(_M1).astype(BF16), jnp.asarray(_M2).astype(BF16),
      jnp.asarray(_M3).astype(BF16), sg.reshape((BATCH,) + digits + (D,)), x1.reshape((BATCH,) + digits + (D,)),
      mod1, w_out.astype(BF16), final_g.reshape(1, D))
    return out.reshape(BATCH, SEQ, D)


def kernel(x, c, ctx, c_ctx, ada_w, ada_b, norm_g, attn_in_w, attn_qn_g, attn_kn_g, lam_q1, lam_k1, lam_q2, lam_k2,
           attn_subln_g, attn_out_w, fourier_in_w, fourier_out_w, final_g):
    assert x.shape == (BATCH, SEQ, D) and ctx.shape == (BATCH, CTX, D)
    assert ada_w.shape[0] == 2 and attn_in_w.shape == (1, D, IN_COLS)
    cond = jnp.concatenate([c, c_ctx[None, :], jnp.zeros((16 - BATCH - 1, D), F32)], axis=0)
    mods = _ada(cond, ada_w, ada_b)
    mod0 = mods[0].reshape(16, 3, D)
    mod1 = mods[1].reshape(16, 3, D)
    lam_init = 0.8 - 0.6 * math.exp(-0.3 * 0)

    k_ctx, vt_ctx, k, vt, q, gate = _inproj(x, ctx, mod0, norm_g[0], attn_in_w[0], attn_qn_g[0], attn_kn_g[0])
    og = _attention(q, k_ctx, vt_ctx, k, vt, gate, lam_q1[0], lam_k1[0], lam_q2[0], lam_k2[0], attn_subln_g[0], lam_init)
    x1, ab, sg = _mid(x, og, mod0, mod1, norm_g[1], attn_out_w[0], fourier_in_w[0])
    return _fourier(ab, sg, x1, mod1, fourier_out_w[0], final_g)
```

```python
import functools
import math

import numpy as np
import jax
import jax.numpy as jnp
from jax import lax
from jax.experimental import pallas as pl
from jax.experimental.pallas import tpu as pltpu

F32 = jnp.float32
BF16 = jnp.bfloat16

D = 1024
BATCH = 8
SEQ = 4096
CTX = 256
GRID_W = 64
ROPE_THETA = 10000.0
EPS = 1e-6
LOG2E = math.log2(math.e)

HD = 128
A_HEADS = 4
A_KV_HEADS = 2
B_HEADS = 4
B_HEAD_DIM = 64
KA0, VA0, KB0, VB0 = 0, 256, 512, 1024
K_COLS = 768
V_ROWS = 768
QA0, QB0, G0 = 1536, 2048, 2560
IN_COLS = 3584
F_GROUPS = 4
F_GROUP_DIM = 256

ROWS = 1024
MID_ROWS = 1024
TQ = 256
TK = 256
VMEM_LIMIT = 56 * 1024 * 1024


def _silu(v):
    return v * (1.0 / (1.0 + jnp.exp(-v)))


def _rms_rows(v):
    return v * lax.rsqrt(jnp.mean(v * v, axis=-1, keepdims=True) + EPS)


def _ada_kernel(c_ref, w_ref, b_ref, o_ref):
    s = _silu(c_ref[...]).astype(BF16)
    o_ref[0] = jnp.dot(s, w_ref[0].astype(BF16), preferred_element_type=F32) + b_ref[0]


def _ada(cond, ada_w, ada_b):
    depth, _, n3 = ada_w.shape
    tn = 1536
    return pl.pallas_call(
        _ada_kernel,
        grid=(depth, n3 // tn),
        in_specs=[
            pl.BlockSpec((16, D), lambda l, j: (0, 0)),
            pl.BlockSpec((1, D, tn), lambda l, j: (l, 0, j)),
            pl.BlockSpec((1, 1, tn), lambda l, j: (l, 0, j)),
        ],
        out_specs=pl.BlockSpec((1, 16, tn), lambda l, j: (l, 0, j)),
        out_shape=jax.ShapeDtypeStruct((depth, 16, n3), F32),
        compiler_params=pltpu.CompilerParams(
            dimension_semantics=("arbitrary", "arbitrary"), vmem_limit_bytes=VMEM_LIMIT),
        name="ada",
    )(cond, ada_w, ada_b.reshape(depth, 1, n3))


def _rope_table(dim, maps):
    q4 = dim // 4
    pos = np.arange(SEQ)
    axis_pos = np.stack([pos // GRID_W, pos % GRID_W], axis=1).astype(np.float64)
    inv = ROPE_THETA ** (-np.arange(q4, dtype=np.float64) / q4)
    ang = axis_pos[:, None, None, :, None] * inv[None, None, None, None, :]
    ang = np.broadcast_to(ang, (SEQ, 2, maps, 2, q4))
    sign = np.array([-1.0, 1.0]).reshape(1, 2, 1, 1, 1)
    cos = np.cos(ang).reshape(SEQ, HD)
    sin = (np.sin(ang) * sign).reshape(SEQ, HD)
    return cos.astype(np.float32), sin.astype(np.float32)


_COS_A, _SIN_A = _rope_table(HD, 1)
_COS_B, _SIN_B = _rope_table(B_HEAD_DIM, 2)


def _rope_order(w, heads, maps):
    q4 = HD // maps // 4
    lead = w.shape[:-1]
    w = w.reshape(lead + (heads, maps, 2, 2, q4))
    n = len(lead)
    w = w.transpose(tuple(range(n)) + (n, n + 3, n + 1, n + 2, n + 4))
    return w.reshape(lead + (heads * HD,))


def _rope(y, cos, sin):
    return y * cos + pltpu.roll(y, HD // 2, 1) * sin


def _hidden(src, g_ref, mod):
    a = g_ref[...] * (1.0 + mod[1:2, :])
    return (_rms_rows(src) * a + mod[0:1, :]).astype(BF16)


def _keys_values(hb, w_ref, wvt_ref, kn_ref, k_ref, rope, rows=slice(None)):
    ka = jnp.dot(hb, w_ref[:, KA0:KA0 + A_KV_HEADS * HD], preferred_element_type=F32)
    for h in range(A_KV_HEADS):
        y = _rms_rows(ka[:, h * HD:(h + 1) * HD]) * kn_ref[...]
        if rope is not None:
            y = _rope(y, rope[0], rope[1])
        k_ref[0, rows, h * HD:(h + 1) * HD] = y.astype(BF16)
    kb = jnp.dot(hb, w_ref[:, KB0:KB0 + B_HEADS * HD], preferred_element_type=F32)
    for h in range(B_HEADS):
        y = kb[:, h * HD:(h + 1) * HD]
        if rope is not None:
            y = _rope(y, rope[2], rope[3])
        k_ref[0, rows, (A_KV_HEADS + h) * HD:(A_KV_HEADS + h + 1) * HD] = y.astype(BF16)
    return lax.dot_general(wvt_ref[...], hb, (((1,), (1,)), ((), ())), preferred_element_type=F32).astype(BF16)


def _ctxproj_kernel(ctx_ref, mod_ref, g_ref, w_ref, wvt_ref, kn_ref, k_ref, vt_ref):
    hb = _hidden(ctx_ref[0], g_ref, mod_ref[0])
    vt_ref[0] = _keys_values(hb, w_ref, wvt_ref, kn_ref, k_ref, None)


def _inproj_kernel(x_ref, mod_ref, g_ref, w_ref, wvt_ref, qn_ref, kn_ref,
                   cosa_ref, sina_ref, cosb_ref, sinb_ref, k_ref, vt_ref, q_ref, gate_ref):
    pieces = [slice(j * TK, (j + 1) * TK) for j in range(ROWS // TK)]
    hbs = [_hidden(x_ref[0, r], g_ref, mod_ref[0]) for r in pieces]
    qgain = qn_ref[...] * (HD ** -0.5 * LOG2E)
    for j, (r, hb) in enumerate(zip(pieces, hbs)):
        rope = (cosa_ref[r, :], sina_ref[r, :], cosb_ref[r, :], sinb_ref[r, :])
        vt_ref[0, j] = _keys_values(hb, w_ref, wvt_ref, kn_ref, k_ref, rope, r)
        qa = jnp.dot(hb, w_ref[:, QA0:QA0 + A_HEADS * HD], preferred_element_type=F32)
        for h in range(A_HEADS):
            y = _rope(_rms_rows(qa[:, h * HD:(h + 1) * HD]) * qgain, rope[0], rope[1])
            q_ref[0, r, h * HD:(h + 1) * HD] = y.astype(BF16)
        qb = jnp.dot(hb, w_ref[:, QB0:QB0 + B_HEADS * HD], preferred_element_type=F32)
        for h in range(B_HEADS):
            y = _rope(qb[:, h * HD:(h + 1) * HD], rope[2], rope[3])
            q_ref[0, r, (A_HEADS + h) * HD:(A_HEADS + h + 1) * HD] = (y * (B_HEAD_DIM ** -0.5 * LOG2E)).astype(BF16)
        gate_ref[0, r] = _silu(jnp.dot(hb, w_ref[:, G0:G0 + D], preferred_element_type=F32)).astype(BF16)


def _inproj(x, ctx, mod, norm_g, w_in, qn_g, kn_g):
    const2 = lambda b, t: (0, 0)
    wb = jnp.concatenate([
        _rope_order(w_in[:, KA0:VA0], A_KV_HEADS, 1), w_in[:, VA0:KB0],
        _rope_order(w_in[:, KB0:VB0], B_HEADS, 2), w_in[:, VB0:QA0],
        _rope_order(w_in[:, QA0:QB0], A_HEADS, 1), _rope_order(w_in[:, QB0:G0], B_HEADS, 2),
        w_in[:, G0:]], axis=1).astype(BF16)
    wvt = jnp.concatenate([w_in[:, VA0:VA0 + A_KV_HEADS * HD].T, w_in[:, VB0:VB0 + B_HEADS * HD].T],
                          axis=0).astype(BF16)
    g2, kn2 = norm_g.reshape(1, D), _rope_order(kn_g, 1, 1).reshape(1, HD)
    params = pltpu.CompilerParams(dimension_semantics=("arbitrary", "arbitrary"), vmem_limit_bytes=VMEM_LIMIT)
    k_ctx, vt_ctx = pl.pallas_call(
        _ctxproj_kernel,
        grid=(BATCH, 1),
        in_specs=[pl.BlockSpec((1, CTX, D), lambda b, t: (b, 0, 0)),
                  pl.BlockSpec((1, 3, D), lambda b, t: (BATCH, 0, 0)),
                  pl.BlockSpec((1, D), const2), pl.BlockSpec((D, IN_COLS), const2),
                  pl.BlockSpec((V_ROWS, D), const2), pl.BlockSpec((1, HD), const2)],
        out_specs=[pl.BlockSpec((1, CTX, K_COLS), lambda b, t: (b, 0, 0)),
                   pl.BlockSpec((1, V_ROWS, CTX), lambda b, t: (b, 0, 0))],
        out_shape=[jax.ShapeDtypeStruct((BATCH, CTX, K_COLS), BF16),
                   jax.ShapeDtypeStruct((BATCH, V_ROWS, CTX), BF16)],
        compiler_params=params,
        name="ctxproj",
    )(ctx, mod, g2, wb, wvt, kn2)

    row = lambda b, t: (b, t, 0)
    table = pl.BlockSpec((ROWS, HD), lambda b, t: (t, 0))
    k, vt, q, gate = pl.pallas_call(
        _inproj_kernel,
        grid=(BATCH, SEQ // ROWS),
        in_specs=[pl.BlockSpec((1, ROWS, D), row),
                  pl.BlockSpec((1, 3, D), lambda b, t: (b, 0, 0)),
                  pl.BlockSpec((1, D), const2), pl.BlockSpec((D, IN_COLS), const2),
                  pl.BlockSpec((V_ROWS, D), const2), pl.BlockSpec((1, HD), const2), pl.BlockSpec((1, HD), const2),
                  table, table, table, table],
        out_specs=[pl.BlockSpec((1, ROWS, K_COLS), row),
                   pl.BlockSpec((1, ROWS // TK, V_ROWS, TK), lambda b, t: (b, t, 0, 0)),
                   pl.BlockSpec((1, ROWS, D), row),
                   pl.BlockSpec((1, ROWS, D), row)],
        out_shape=[jax.ShapeDtypeStruct((BATCH, SEQ, K_COLS), BF16),
                   jax.ShapeDtypeStruct((BATCH, SEQ // TK, V_ROWS, TK), BF16),
                   jax.ShapeDtypeStruct((BATCH, SEQ, D), BF16),
                   jax.ShapeDtypeStruct((BATCH, SEQ, D), BF16)],
        compiler_params=params,
        name="inproj",
    )(x, mod, g2, wb, wvt, _rope_order(qn_g, 1, 1).reshape(1, HD), kn2, _COS_A, _SIN_A, _COS_B, _SIN_B)
    return k_ctx, vt_ctx, k, vt, q, gate


N_CHAINS = A_KV_HEADS + B_HEADS
ACC_ROWS = HD + 16
CHUNKS_PER_ITER = 8


def _attn_kernel(lq1_ref, lk1_ref, lq2_ref, lk2_ref, subln_ref, q_ref, kc_ref, vtc_ref, k_ref, vt_ref, gate_ref, o_ref,
                 acc_ref, m_ref, s_ref, *, lam_init):
    nq = q_ref.shape[1]
    group = A_HEADS // A_KV_HEADS
    lane = lax.broadcasted_iota(jnp.int32, (nq, HD), 1)

    q2 = []
    for kh in range(A_KV_HEADS):
        q2.append(jnp.concatenate([q_ref[0, :, (kh * group + g) * HD:(kh * group + g + 1) * HD]
                                   for g in range(group)], axis=0))
    for h in range(B_HEADS):
        qh = q_ref[0, :, (A_HEADS + h) * HD:(A_HEADS + h + 1) * HD]
        zero = jnp.zeros_like(qh)
        map1 = (lane & (HD // 4)) == 0
        q2.append(jnp.concatenate([jnp.where(map1, qh, zero), jnp.where(map1, zero, qh)], axis=0))

    ones = jnp.concatenate([jnp.ones((1, TK), BF16), jnp.zeros((ACC_ROWS - HD - 1, TK), BF16)], axis=0)
    n_chunks = SEQ // TK

    def scores(c, ci):
        if c is None:
            k = kc_ref[0, :, ci * HD:(ci + 1) * HD]
        else:
            k = k_ref[0, pl.ds(pl.multiple_of(c * TK, TK), TK), ci * HD:(ci + 1) * HD]
        return lax.dot_general(k, q2[ci], (((1,), (1,)), ((), ())), preferred_element_type=F32)

    def stage_scores(buf, c):
        for ci in range(N_CHAINS):
            s_ref[buf, ci] = scores(c, ci)

    def consume(buf, c):
        first = c is None
        for ci in range(N_CHAINS):
            st = s_ref[buf, ci]
            v_t = vtc_ref[0, ci * HD:(ci + 1) * HD, :] if first else vt_ref[0, c, ci * HD:(ci + 1) * HD, :]
            vt = jnp.concatenate([v_t, ones], axis=0)
            m_new = jnp.max(st, axis=0, keepdims=True)
            if not first:
                m_old = m_ref[ci]
                m_new = jnp.maximum(m_old, m_new)
            pv = jnp.dot(vt, jnp.exp2(st - m_new).astype(BF16), preferred_element_type=F32)
            acc_ref[ci] = pv if first else jnp.exp2(m_old - m_new) * acc_ref[ci] + pv
            m_ref[ci] = m_new

    def chunks(i, carry):
        for j in range(CHUNKS_PER_ITER):
            c = CHUNKS_PER_ITER * i + j
            stage_scores(j % 2, jnp.minimum(c + 1, n_chunks - 1))
            consume((j + 1) % 2, c)
        return carry

    stage_scores(0, None)
    stage_scores(1, 0)
    consume(0, None)
    lax.fori_loop(0, n_chunks // CHUNKS_PER_ITER, chunks, 0)

    def normalized(ci):
        acc = acc_ref[ci]
        return acc[0:HD] * (1.0 / acc[HD:HD + 1])

    for kh in range(A_KV_HEADS):
        ot = normalized(kh)
        for g in range(group):
            h = kh * group + g
            o = ot[:, g * nq:(g + 1) * nq].T
            o_ref[0, :, h * HD:(h + 1) * HD] = (o * gate_ref[0, :, h * HD:(h + 1) * HD]).astype(BF16)

    lam = (jnp.exp(jnp.sum(lq1_ref[...] * lk1_ref[...], keepdims=True))
           - jnp.exp(jnp.sum(lq2_ref[...] * lk2_ref[...], keepdims=True)) + lam_init)
    for h in range(B_HEADS):
        c0 = (A_HEADS + h) * HD
        ot = normalized(A_KV_HEADS + h)
        od = ot[:, 0:nq] - lam * ot[:, nq:2 * nq]
        od = od * lax.rsqrt(jnp.mean(od * od, axis=0, keepdims=True) + EPS)
        o = od.T * subln_ref[...] * (1.0 - lam_init)
        o_ref[0, :, c0:c0 + HD] = (o * gate_ref[0, :, c0:c0 + HD]).astype(BF16)


def _attention(q, k_ctx, vt_ctx, k, vt, gate, lq1, lk1, lq2, lk2, subln_g, lam_init):
    assert CTX == TK
    const2 = lambda b, i: (0, 0)
    lamspec = pl.BlockSpec((1, B_HEAD_DIM), const2)
    qspec = pl.BlockSpec((1, TQ, D), lambda b, i: (b, i, 0))
    return pl.pallas_call(
        functools.partial(_attn_kernel, lam_init=lam_init),
        grid=(BATCH, SEQ // TQ),
        in_specs=[lamspec, lamspec, lamspec, lamspec,
                  pl.BlockSpec((1, HD), const2),
                  qspec,
                  pl.BlockSpec((1, CTX, K_COLS), lambda b, i: (b, 0, 0)),
                  pl.BlockSpec((1, V_ROWS, CTX), lambda b, i: (b, 0, 0)),
                  pl.BlockSpec((1, SEQ, K_COLS), lambda b, i: (b, 0, 0)),
                  pl.BlockSpec((1, SEQ // TK, V_ROWS, TK), lambda b, i: (b, 0, 0, 0)),
                  qspec],
        out_specs=qspec,
        out_shape=jax.ShapeDtypeStruct((BATCH, SEQ, D), BF16),
        scratch_shapes=[pltpu.VMEM((N_CHAINS, ACC_ROWS, 2 * TQ), F32),
                        pltpu.VMEM((N_CHAINS, 1, 2 * TQ), F32),
                        pltpu.VMEM((2, N_CHAINS, TK, 2 * TQ), F32)],
        compiler_params=pltpu.CompilerParams(
            dimension_semantics=("arbitrary", "arbitrary"), vmem_limit_bytes=VMEM_LIMIT),
        name="attn",
    )(lq1.reshape(1, -1), lk1.reshape(1, -1), lq2.reshape(1, -1), lk2.reshape(1, -1),
      subln_g.reshape(1, HD), q, k_ctx, vt_ctx, k, vt, gate)


def _channel_dft():
    c = np.arange(F_GROUP_DIM)
    ang = 2.0 * np.pi * ((c[:, None] * c[None, :]) % F_GROUP_DIM) / F_GROUP_DIM
    return (np.concatenate([np.cos(ang), np.sin(ang)], axis=1) / 16.0).astype(np.float32)


_CHANNEL_DFT = _channel_dft()


def _mid_kernel(x_ref, og_ref, mod0_ref, mod1_ref, g_ref, wo_ref, wf_ref, cs_ref, x1_ref, ab_ref, sg_ref):
    mod = mod1_ref[0]
    pieces = [slice(h * TK, (h + 1) * TK) for h in range(MID_ROWS // TK)]

    def out_proj(r):
        return jnp.dot(og_ref[0, r], wo_ref[...], preferred_element_type=F32)

    def channel_dft(r, u):
        for gi in range(F_GROUPS):
            c0 = gi * F_GROUP_DIM
            ab = jnp.dot(u[:, c0:c0 + F_GROUP_DIM], cs_ref[...], preferred_element_type=F32)
            ab_ref[0, 0, r, c0:c0 + F_GROUP_DIM] = ab[:, 0:F_GROUP_DIM].astype(BF16)
            ab_ref[0, 1, r, c0:c0 + F_GROUP_DIM] = ab[:, F_GROUP_DIM:].astype(BF16)

    a_next, pending = out_proj(pieces[0]), None
    for p, r in enumerate(pieces):
        a = a_next
        if p + 1 < len(pieces):
            a_next = out_proj(pieces[p + 1])
        x1 = x_ref[0, r] + mod0_ref[0, 2:3, :] * a
        x1_ref[0, r] = x1
        hb = (_rms_rows(x1) * (g_ref[...] * (1.0 + mod[1:2, :])) + mod[0:1, :]).astype(BF16)
        u = jnp.dot(hb, wf_ref[:, 0:D], preferred_element_type=F32).astype(BF16)
        sg_ref[0, r] = _silu(jnp.dot(hb, wf_ref[:, D:2 * D], preferred_element_type=F32)).astype(BF16)
        if pending is not None:
            channel_dft(*pending)
        pending = (r, u)
    channel_dft(*pending)


def _mid(x, og, mod0, mod1, norm_g, w_out, w_f):
    const2 = lambda b, i: (0, 0)
    row = lambda b, i: (b, i, 0)
    modspec = pl.BlockSpec((1, 3, D), lambda b, i: (b, 0, 0))
    return pl.pallas_call(
        _mid_kernel,
        grid=(BATCH, SEQ // MID_ROWS),
        in_specs=[pl.BlockSpec((1, MID_ROWS, D), row), pl.BlockSpec((1, MID_ROWS, D), row), modspec, modspec,
                  pl.BlockSpec((1, D), const2), pl.BlockSpec((D, D), const2), pl.BlockSpec((D, 2 * D), const2),
                  pl.BlockSpec((F_GROUP_DIM, 2 * F_GROUP_DIM), const2)],
        out_specs=[pl.BlockSpec((1, MID_ROWS, D), row),
                   pl.BlockSpec((1, 2, MID_ROWS, D), lambda b, i: (b, 0, i, 0)),
                   pl.BlockSpec((1, MID_ROWS, D), row)],
        out_shape=[jax.ShapeDtypeStruct((BATCH, SEQ, D), F32),
                   jax.ShapeDtypeStruct((BATCH, 2, SEQ, D), BF16),
                   jax.ShapeDtypeStruct((BATCH, SEQ, D), BF16)],
        compiler_params=pltpu.CompilerParams(
            dimension_semantics=("arbitrary", "arbitrary"), vmem_limit_bytes=VMEM_LIMIT),
        name="mid",
    )(x, og, mod0, mod1, norm_g.reshape(1, D), w_out.astype(BF16), w_f.astype(BF16),
      jnp.asarray(_CHANNEL_DFT).astype(BF16))


RADIX = 16
FBLK = RADIX * RADIX
STAGE_BLOCKS = 4
STAGE_STEPS = RADIX // STAGE_BLOCKS


def _stage_matrices():
    w = lambda e: np.exp(-2j * np.pi * (e % SEQ) / SEQ)
    i = np.arange(RADIX)
    eye = np.eye(RADIX)
    g = np.einsum("cd,kac->dkac", eye, w(256 * i[:, None, None] * i[None, :, None] + i[:, None, None] * i[None, None, :]))
    g = g.reshape(FBLK, FBLK)
    m1 = np.block([[g.real, g.imag], [g.imag, -g.real]])
    h = np.einsum("jk,qbk->qjbk", eye, w(256 * i[:, None, None] * i[None, :, None] + 16 * i[None, None, :] * i[None, :, None]))
    h = h.reshape(FBLK, FBLK)
    m2 = np.block([[h.real, -h.imag], [h.imag, h.real]])
    ph = w(i[None, None, :] * (16 * i[:, None, None] + 256 * i[None, :, None]))
    e = np.einsum("jk,bqc->bqjck", eye, ph).reshape(RADIX, FBLK, FBLK)
    m3 = np.concatenate([e.real, -e.imag], axis=2) / 64.0
    return m1.astype(np.float32), m2.astype(np.float32), m3.astype(np.float32)


_M1, _M2, _M3 = _stage_matrices()


def _fourier_kernel(ab_ref, m1_ref, m2_ref, m3_ref, sg_ref, x1_ref, mod_ref, wo_ref, fg_ref, o_ref, z_ref):
    s = pl.program_id(1)

    @pl.when(s < STAGE_STEPS)
    def _():
        for i in range(STAGE_BLOCKS):
            b = s * STAGE_BLOCKS + i
            rhs = ab_ref[0, :, :, i].reshape(2 * FBLK, D)
            y = jnp.dot(m1_ref[...], rhs, preferred_element_type=F32).astype(BF16)
            z_ref[:, :, pl.ds(b, 1)] = y.reshape(2, RADIX, 1, RADIX, D)

    @pl.when((s >= STAGE_STEPS) & (s < 2 * STAGE_STEPS))
    def _():
        for i in range(STAGE_BLOCKS):
            c = (s - STAGE_STEPS) * STAGE_BLOCKS + i
            rhs = z_ref[:, pl.ds(c, 1)].reshape(2 * FBLK, D)
            y = jnp.dot(m2_ref[...], rhs, preferred_element_type=F32).astype(BF16)
            z_ref[:, pl.ds(c, 1)] = y.reshape(2, 1, RADIX, RADIX, D)

    @pl.when(s >= 2 * STAGE_STEPS)
    def _():
        def stage3(i):
            kb = (s - 2 * STAGE_STEPS) * STAGE_BLOCKS + i
            rhs = z_ref[:, :, pl.ds(kb, 1)].reshape(2 * FBLK, D)
            return jnp.dot(m3_ref[i], rhs, preferred_element_type=F32)

        f_next = stage3(0)
        for i in range(STAGE_BLOCKS):
            f = f_next
            if i + 1 < STAGE_BLOCKS:
                f_next = stage3(i + 1)
            fg = (f * sg_ref[0, :, i].reshape(FBLK, D).astype(F32)).astype(BF16)
            x2 = x1_ref[0, :, i].reshape(FBLK, D) + mod_ref[0, 2:3, :] * jnp.dot(fg, wo_ref[...], preferred_element_type=F32)
            o_ref[0, :, i] = (_rms_rows(x2) * fg_ref[...]).reshape(RADIX, RADIX, D)


def _fourier(ab, sg, x1, mod1, w_out, final_g):
    digits = (RADIX, RADIX, RADIX)
    const2 = lambda b, s: (0, 0)
    last = lambda s: jnp.clip(s - 2 * STAGE_STEPS, 0, STAGE_STEPS - 1)
    rows3 = pl.BlockSpec((1, RADIX, STAGE_BLOCKS, RADIX, D), lambda b, s: (b, 0, last(s), 0, 0))
    out = pl.pallas_call(
        _fourier_kernel,
        grid=(BATCH, 3 * STAGE_STEPS),
        in_specs=[pl.BlockSpec((1, 2, RADIX, STAGE_BLOCKS, RADIX, D),
                               lambda b, s: (b, 0, 0, jnp.minimum(s, STAGE_STEPS - 1), 0, 0)),
                  pl.BlockSpec((2 * FBLK, 2 * FBLK), const2),
                  pl.BlockSpec((2 * FBLK, 2 * FBLK), const2),
                  pl.BlockSpec((STAGE_BLOCKS, FBLK, 2 * FBLK), lambda b, s: (last(s), 0, 0)),
                  rows3, rows3,
                  pl.BlockSpec((1, 3, D), lambda b, s: (b, 0, 0)),
                  pl.BlockSpec((D, D), const2), pl.BlockSpec((1, D), const2)],
        out_specs=rows3,
        out_shape=jax.ShapeDtypeStruct((BATCH,) + digits + (D,), F32),
        scratch_shapes=[pltpu.VMEM((2,) + digits + (D,), BF16)],
        compiler_params=pltpu.CompilerParams(
            dimension_semantics=("arbitrary", "arbitrary"), vmem_limit_bytes=VMEM_LIMIT),
        name="fourier",
    )(ab.reshape((BATCH, 2) + digits + (D,)), jnp.asarray(_M1).astype(BF16), jnp.asarray(_M2).astype(BF16),
      jnp.asarray(_M3).astype(BF16), sg.reshape((BATCH,) + digits + (D,)), x1.reshape((BATCH,) + digits + (D,)),
      mod1, w_out.astype(BF16), final_g.reshape(1, D))
    return out.reshape(BATCH, SEQ, D)


def kernel(x, c, ctx, c_ctx, ada_w, ada_b, norm_g, attn_in_w, attn_qn_g, attn_kn_g, lam_q1, lam_k1, lam_q2, lam_k2,
           attn_subln_g, attn_out_w, fourier_in_w, fourier_out_w, final_g):
    assert x.shape == (BATCH, SEQ, D) and ctx.shape == (BATCH, CTX, D)
    assert ada_w.shape[0] == 2 and attn_in_w.shape == (1, D, IN_COLS)
    cond = jnp.concatenate([c, c_ctx[None, :], jnp.zeros((16 - BATCH - 1, D), F32)], axis=0)
    mods = _ada(cond, ada_w, ada_b)
    mod0 = mods[0].reshape(16, 3, D)
    mod1 = mods[1].reshape(16, 3, D)
    lam_init = 0.8 - 0.6 * math.exp(-0.3 * 0)

    k_ctx, vt_ctx, k, vt, q, gate = _inproj(x, ctx, mod0, norm_g[0], attn_in_w[0], attn_qn_g[0], attn_kn_g[0])
    og = _attention(q, k_ctx, vt_ctx, k, vt, gate, lam_q1[0], lam_k1[0], lam_q2[0], lam_k2[0], attn_subln_g[0], lam_init)
    x1, ab, sg = _mid(x, og, mod0, mod1, norm_g[1], attn_out_w[0], fourier_in_w[0])
    return _fourier(ab, sg, x1, mod1, fourier_out_w[0], final_g)
```

```python
import functools
import math

import numpy as np
import jax
import jax.numpy as jnp
from jax import lax
from jax.experimental import pallas as pl
from jax.experimental.pallas import tpu as pltpu

F32 = jnp.float32
BF16 = jnp.bfloat16

D = 1024
BATCH = 8
SEQ = 4096
CTX = 256
GRID_W = 64
ROPE_THETA = 10000.0
EPS = 1e-6
LOG2E = math.log2(math.e)

HD = 128
A_HEADS = 4
A_KV_HEADS = 2
B_HEADS = 4
B_HEAD_DIM = 64
KA0, VA0, KB0, VB0 = 0, 256, 512, 1024
K_COLS = 768
V_ROWS = 768
QA0, QB0, G0 = 1536, 2048, 2560
IN_COLS = 3584
F_GROUPS = 4
F_GROUP_DIM = 256

ROWS = 1024
MID_ROWS = 1024
TQ = 256
TK = 256
VMEM_LIMIT = 56 * 1024 * 1024


def _silu(v):
    return v * (1.0 / (1.0 + jnp.exp(-v)))


def _rms_rows(v):
    return v * lax.rsqrt(jnp.mean(v * v, axis=-1, keepdims=True) + EPS)


def _ada_kernel(c_ref, w_ref, b_ref, o_ref):
    s = _silu(c_ref[...]).astype(BF16)
    o_ref[0] = jnp.dot(s, w_ref[0].astype(BF16), preferred_element_type=F32) + b_ref[0]


def _ada(cond, ada_w, ada_b):
    depth, _, n3 = ada_w.shape
    tn = 1536
    return pl.pallas_call(
        _ada_kernel,
        grid=(depth, n3 // tn),
        in_specs=[
            pl.BlockSpec((16, D), lambda l, j: (0, 0)),
            pl.BlockSpec((1, D, tn), lambda l, j: (l, 0, j)),
            pl.BlockSpec((1, 1, tn), lambda l, j: (l, 0, j)),
        ],
        out_specs=pl.BlockSpec((1, 16, tn), lambda l, j: (l, 0, j)),
        out_shape=jax.ShapeDtypeStruct((depth, 16, n3), F32),
        compiler_params=pltpu.CompilerParams(
            dimension_semantics=("arbitrary", "arbitrary"), vmem_limit_bytes=VMEM_LIMIT),
        name="ada",
    )(cond, ada_w, ada_b.reshape(depth, 1, n3))


def _rope_table(dim, maps):
    q4 = dim // 4
    pos = np.arange(SEQ)
    axis_pos = np.stack([pos // GRID_W, pos % GRID_W], axis=1).astype(np.float64)
    inv = ROPE_THETA ** (-np.arange(q4, dtype=np.float64) / q4)
    ang = axis_pos[:, None, None, :, None] * inv[None, None, None, None, :]
    ang = np.broadcast_to(ang, (SEQ, 2, maps, 2, q4))
    sign = np.array([-1.0, 1.0]).reshape(1, 2, 1, 1, 1)
    cos = np.cos(ang).reshape(SEQ, HD)
    sin = (np.sin(ang) * sign).reshape(SEQ, HD)
    return cos.astype(np.float32), sin.astype(np.float32)


_COS_A, _SIN_A = _rope_table(HD, 1)
_COS_B, _SIN_B = _rope_table(B_HEAD_DIM, 2)


def _rope_order(w, heads, maps):
    q4 = HD // maps // 4
    lead = w.shape[:-1]
    w = w.reshape(lead + (heads, maps, 2, 2, q4))
    n = len(lead)
    w = w.transpose(tuple(range(n)) + (n, n + 3, n + 1, n + 2, n + 4))
    return w.reshape(lead + (heads * HD,))


def _rope(y, cos, sin):
    return y * cos + pltpu.roll(y, HD // 2, 1) * sin


def _hidden(src, g_ref, mod):
    a = g_ref[...] * (1.0 + mod[1:2, :])
    return (_rms_rows(src) * a + mod[0:1, :]).astype(BF16)


def _keys_values(hb, w_ref, wvt_ref, kn_ref, k_ref, rope, rows=slice(None)):
    ka = jnp.dot(hb, w_ref[:, KA0:KA0 + A_KV_HEADS * HD], preferred_element_type=F32)
    for h in range(A_KV_HEADS):
        y = _rms_rows(ka[:, h * HD:(h + 1) * HD]) * kn_ref[...]
        if rope is not None:
            y = _rope(y, rope[0], rope[1])
        k_ref[0, rows, h * HD:(h + 1) * HD] = y.astype(BF16)
    kb = jnp.dot(hb, w_ref[:, KB0:KB0 + B_HEADS * HD], preferred_element_type=F32)
    for h in range(B_HEADS):
        y = kb[:, h * HD:(h + 1) * HD]
        if rope is not None:
            y = _rope(y, rope[2], rope[3])
        k_ref[0, rows, (A_KV_HEADS + h) * HD:(A_KV_HEADS + h + 1) * HD] = y.astype(BF16)
    return lax.dot_general(wvt_ref[...], hb, (((1,), (1,)), ((), ())), preferred_element_type=F32).astype(BF16)


def _ctxproj_kernel(ctx_ref, mod_ref, g_ref, w_ref, wvt_ref, kn_ref, k_ref, vt_ref):
    hb = _hidden(ctx_ref[0], g_ref, mod_ref[0])
    vt_ref[0] = _keys_values(hb, w_ref, wvt_ref, kn_ref, k_ref, None)


def _inproj_kernel(x_ref, mod_ref, g_ref, w_ref, wvt_ref, qn_ref, kn_ref,
                   cosa_ref, sina_ref, cosb_ref, sinb_ref, k_ref, vt_ref, q_ref, gate_ref):
    pieces = [slice(j * TK, (j + 1) * TK) for j in range(ROWS // TK)]
    hbs = [_hidden(x_ref[0, r], g_ref, mod_ref[0]) for r in pieces]
    qgain = qn_ref[...] * (HD ** -0.5 * LOG2E)
    for j, (r, hb) in enumerate(zip(pieces, hbs)):
        rope = (cosa_ref[r, :], sina_ref[r, :], cosb_ref[r, :], sinb_ref[r, :])
        vt_ref[0, j] = _keys_values(hb, w_ref, wvt_ref, kn_ref, k_ref, rope, r)
        qa = jnp.dot(hb, w_ref[:, QA0:QA0 + A_HEADS * HD], preferred_element_type=F32)
        for h in range(A_HEADS):
            y = _rope(_rms_rows(qa[:, h * HD:(h + 1) * HD]) * qgain, rope[0], rope[1])
            q_ref[0, r, h * HD:(h + 1) * HD] = y.astype(BF16)
        qb = jnp.dot(hb, w_ref[:, QB0:QB0 + B_HEADS * HD], preferred_element_type=F32)
        for h in range(B_HEADS):
            y = _rope(qb[:, h * HD:(h + 1) * HD], rope[2], rope[3])
            q_ref[0, r, (A_HEADS + h) * HD:(A_HEADS + h + 1) * HD] = (y * (B_HEAD_DIM ** -0.5 * LOG2E)).astype(BF16)
        gate_ref[0, r] = _silu(jnp.dot(hb, w_ref[:, G0:G0 + D], preferred_element_type=F32)).astype(BF16)


def _inproj(x, ctx, mod, norm_g, w_in, qn_g, kn_g):
    const2 = lambda b, t: (0, 0)
    wb = jnp.concatenate([
        _rope_order(w_in[:, KA0:VA0], A_KV_HEADS, 1), w_in[:, VA0:KB0],
        _rope_order(w_in[:, KB0:VB0], B_HEADS, 2), w_in[:, VB0:QA0],
        _rope_order(w_in[:, QA0:QB0], A_HEADS, 1), _rope_order(w_in[:, QB0:G0], B_HEADS, 2),
        w_in[:, G0:]], axis=1).astype(BF16)
    wvt = jnp.concatenate([w_in[:, VA0:VA0 + A_KV_HEADS * HD].T, w_in[:, VB0:VB0 + B_HEADS * HD].T],
                          axis=0).astype(BF16)
    g2, kn2 = norm_g.reshape(1, D), _rope_order(kn_g, 1, 1).reshape(1, HD)
    params = pltpu.CompilerParams(dimension_semantics=("arbitrary", "arbitrary"), vmem_limit_bytes=VMEM_LIMIT)
    k_ctx, vt_ctx = pl.pallas_call(
        _ctxproj_kernel,
        grid=(BATCH, 1),
        in_specs=[pl.BlockSpec((1, CTX, D), lambda b, t: (b, 0, 0)),
                  pl.BlockSpec((1, 3, D), lambda b, t: (BATCH, 0, 0)),
                  pl.BlockSpec((1, D), const2), pl.BlockSpec((D, IN_COLS), const2),
                  pl.BlockSpec((V_ROWS, D), const2), pl.BlockSpec((1, HD), const2)],
        out_specs=[pl.BlockSpec((1, CTX, K_COLS), lambda b, t: (b, 0, 0)),
                   pl.BlockSpec((1, V_ROWS, CTX), lambda b, t: (b, 0, 0))],
        out_shape=[jax.ShapeDtypeStruct((BATCH, CTX, K_COLS), BF16),
                   jax.ShapeDtypeStruct((BATCH, V_ROWS, CTX), BF16)],
        compiler_params=params,
        name="ctxproj",
    )(ctx, mod, g2, wb, wvt, kn2)

    row = lambda b, t: (b, t, 0)
    table = pl.BlockSpec((ROWS, HD), lambda b, t: (t, 0))
    k, vt, q, gate = pl.pallas_call(
        _inproj_kernel,
        grid=(BATCH, SEQ // ROWS),
        in_specs=[pl.BlockSpec((1, ROWS, D), row),
                  pl.BlockSpec((1, 3, D), lambda b, t: (b, 0, 0)),
                  pl.BlockSpec((1, D), const2), pl.BlockSpec((D, IN_COLS), const2, pipeline_mode=pl.Buffered(1)),
                  pl.BlockSpec((V_ROWS, D), const2, pipeline_mode=pl.Buffered(1)),
                  pl.BlockSpec((1, HD), const2), pl.BlockSpec((1, HD), const2),
                  table, table, table, table],
        out_specs=[pl.BlockSpec((1, ROWS, K_COLS), row),
                   pl.BlockSpec((1, ROWS // TK, V_ROWS, TK), lambda b, t: (b, t, 0, 0)),
                   pl.BlockSpec((1, ROWS, D), row),
                   pl.BlockSpec((1, ROWS, D), row)],
        out_shape=[jax.ShapeDtypeStruct((BATCH, SEQ, K_COLS), BF16),
                   jax.ShapeDtypeStruct((BATCH, SEQ // TK, V_ROWS, TK), BF16),
                   jax.ShapeDtypeStruct((BATCH, SEQ, D), BF16),
                   jax.ShapeDtypeStruct((BATCH, SEQ, D), BF16)],
        compiler_params=params,
        name="inproj",
    )(x, mod, g2, wb, wvt, _rope_order(qn_g, 1, 1).reshape(1, HD), kn2, _COS_A, _SIN_A, _COS_B, _SIN_B)
    return k_ctx, vt_ctx, k, vt, q, gate


N_CHAINS = A_KV_HEADS + B_HEADS
ACC_ROWS = HD + 16
CHUNKS_PER_ITER = 8


def _attn_kernel(lq1_ref, lk1_ref, lq2_ref, lk2_ref, subln_ref, q_ref, kc_ref, vtc_ref, k_ref, vt_ref, gate_ref, o_ref,
                 acc_ref, m_ref, s_ref, *, lam_init):
    nq = q_ref.shape[1]
    group = A_HEADS // A_KV_HEADS
    lane = lax.broadcasted_iota(jnp.int32, (nq, HD), 1)

    q2 = []
    for kh in range(A_KV_HEADS):
        q2.append(jnp.concatenate([q_ref[0, :, (kh * group + g) * HD:(kh * group + g + 1) * HD]
                                   for g in range(group)], axis=0))
    for h in range(B_HEADS):
        qh = q_ref[0, :, (A_HEADS + h) * HD:(A_HEADS + h + 1) * HD]
        zero = jnp.zeros_like(qh)
        map1 = (lane & (HD // 4)) == 0
        q2.append(jnp.concatenate([jnp.where(map1, qh, zero), jnp.where(map1, zero, qh)], axis=0))

    ones = jnp.concatenate([jnp.ones((1, TK), BF16), jnp.zeros((ACC_ROWS - HD - 1, TK), BF16)], axis=0)
    n_chunks = SEQ // TK

    def scores(c, ci):
        if c is None:
            k = kc_ref[0, :, ci * HD:(ci + 1) * HD]
        else:
            k = k_ref[0, pl.ds(pl.multiple_of(c * TK, TK), TK), ci * HD:(ci + 1) * HD]
        return lax.dot_general(k, q2[ci], (((1,), (1,)), ((), ())), preferred_element_type=F32)

    def stage_scores(buf, c):
        for ci in range(N_CHAINS):
            s_ref[buf, ci] = scores(c, ci)

    def consume(buf, c):
        first = c is None
        for ci in range(N_CHAINS):
            st = s_ref[buf, ci]
            v_t = vtc_ref[0, ci * HD:(ci + 1) * HD, :] if first else vt_ref[0, c, ci * HD:(ci + 1) * HD, :]
            vt = jnp.concatenate([v_t, ones], axis=0)
            m_new = jnp.max(st, axis=0, keepdims=True)
            if not first:
                m_old = m_ref[ci]
                m_new = jnp.maximum(m_old, m_new)
            pv = jnp.dot(vt, jnp.exp2(st - m_new).astype(BF16), preferred_element_type=F32)
            acc_ref[ci] = pv if first else jnp.exp2(m_old - m_new) * acc_ref[ci] + pv
            m_ref[ci] = m_new

    def chunks(i, carry):
        for j in range(CHUNKS_PER_ITER):
            c = CHUNKS_PER_ITER * i + j
            stage_scores(j % 2, jnp.minimum(c + 1, n_chunks - 1))
            consume((j + 1) % 2, c)
        return carry

    stage_scores(0, None)
    stage_scores(1, 0)
    consume(0, None)
    lax.fori_loop(0, n_chunks // CHUNKS_PER_ITER, chunks, 0)

    def normalized(ci):
        acc = acc_ref[ci]
        return acc[0:HD] * (1.0 / acc[HD:HD + 1])

    for kh in range(A_KV_HEADS):
        ot = normalized(kh)
        for g in range(group):
            h = kh * group + g
            o = ot[:, g * nq:(g + 1) * nq].T
            o_ref[0, :, h * HD:(h + 1) * HD] = (o * gate_ref[0, :, h * HD:(h + 1) * HD]).astype(BF16)

    lam = (jnp.exp(jnp.sum(lq1_ref[...] * lk1_ref[...], keepdims=True))
           - jnp.exp(jnp.sum(lq2_ref[...] * lk2_ref[...], keepdims=True)) + lam_init)
    for h in range(B_HEADS):
        c0 = (A_HEADS + h) * HD
        ot = normalized(A_KV_HEADS + h)
        od = ot[:, 0:nq] - lam * ot[:, nq:2 * nq]
        od = od * lax.rsqrt(jnp.mean(od * od, axis=0, keepdims=True) + EPS)
        o = od.T * subln_ref[...] * (1.0 - lam_init)
        o_ref[0, :, c0:c0 + HD] = (o * gate_ref[0, :, c0:c0 + HD]).astype(BF16)


def _attention(q, k_ctx, vt_ctx, k, vt, gate, lq1, lk1, lq2, lk2, subln_g, lam_init):
    assert CTX == TK
    const2 = lambda b, i: (0, 0)
    lamspec = pl.BlockSpec((1, B_HEAD_DIM), const2)
    qspec = pl.BlockSpec((1, TQ, D), lambda b, i: (b, i, 0))
    return pl.pallas_call(
        functools.partial(_attn_kernel, lam_init=lam_init),
        grid=(BATCH, SEQ // TQ),
        in_specs=[lamspec, lamspec, lamspec, lamspec,
                  pl.BlockSpec((1, HD), const2),
                  qspec,
                  pl.BlockSpec((1, CTX, K_COLS), lambda b, i: (b, 0, 0)),
                  pl.BlockSpec((1, V_ROWS, CTX), lambda b, i: (b, 0, 0)),
                  pl.BlockSpec((1, SEQ, K_COLS), lambda b, i: (b, 0, 0)),
                  pl.BlockSpec((1, SEQ // TK, V_ROWS, TK), lambda b, i: (b, 0, 0, 0)),
                  qspec],
        out_specs=qspec,
        out_shape=jax.ShapeDtypeStruct((BATCH, SEQ, D), BF16),
        scratch_shapes=[pltpu.VMEM((N_CHAINS, ACC_ROWS, 2 * TQ), F32),
                        pltpu.VMEM((N_CHAINS, 1, 2 * TQ), F32),
                        pltpu.VMEM((2, N_CHAINS, TK, 2 * TQ), F32)],
        compiler_params=pltpu.CompilerParams(
            dimension_semantics=("arbitrary", "arbitrary"), vmem_limit_bytes=VMEM_LIMIT),
        name="attn",
    )(lq1.reshape(1, -1), lk1.reshape(1, -1), lq2.reshape(1, -1), lk2.reshape(1, -1),
      subln_g.reshape(1, HD), q, k_ctx, vt_ctx, k, vt, gate)


def _channel_dft():
    c = np.arange(F_GROUP_DIM)
    ang = 2.0 * np.pi * ((c[:, None] * c[None, :]) % F_GROUP_DIM) / F_GROUP_DIM
    return (np.concatenate([np.cos(ang), np.sin(ang)], axis=1) / 16.0).astype(np.float32)


_CHANNEL_DFT = _channel_dft()


def _mid_kernel(x_ref, og_ref, mod0_ref, mod1_ref, g_ref, wo_ref, wf_ref, cs_ref, x1_ref, ab_ref, sg_ref):
    mod = mod1_ref[0]
    pieces = [slice(h * TK, (h + 1) * TK) for h in range(MID_ROWS // TK)]

    def out_proj(r):
        return jnp.dot(og_ref[0, r], wo_ref[...], preferred_element_type=F32)

    def channel_dft(r, u):
        for gi in range(F_GROUPS):
            c0 = gi * F_GROUP_DIM
            ab = jnp.dot(u[:, c0:c0 + F_GROUP_DIM], cs_ref[...], preferred_element_type=F32)
            ab_ref[0, 0, r, c0:c0 + F_GROUP_DIM] = ab[:, 0:F_GROUP_DIM].astype(BF16)
            ab_ref[0, 1, r, c0:c0 + F_GROUP_DIM] = ab[:, F_GROUP_DIM:].astype(BF16)

    a_next, pending = out_proj(pieces[0]), None
    for p, r in enumerate(pieces):
        a = a_next
        if p + 1 < len(pieces):
            a_next = out_proj(pieces[p + 1])
        x1 = x_ref[0, r] + mod0_ref[0, 2:3, :] * a
        x1_ref[0, r] = x1
        hb = (_rms_rows(x1) * (g_ref[...] * (1.0 + mod[1:2, :])) + mod[0:1, :]).astype(BF16)
        u = jnp.dot(hb, wf_ref[:, 0:D], preferred_element_type=F32).astype(BF16)
        sg_ref[0, r] = _silu(jnp.dot(hb, wf_ref[:, D:2 * D], preferred_element_type=F32)).astype(BF16)
        if pending is not None:
            channel_dft(*pending)
        pending = (r, u)
    channel_dft(*pending)


def _mid(x, og, mod0, mod1, norm_g, w_out, w_f):
    const2 = lambda b, i: (0, 0)
    row = lambda b, i: (b, i, 0)
    modspec = pl.BlockSpec((1, 3, D), lambda b, i: (b, 0, 0))
    return pl.pallas_call(
        _mid_kernel,
        grid=(BATCH, SEQ // MID_ROWS),
        in_specs=[pl.BlockSpec((1, MID_ROWS, D), row),
                  pl.BlockSpec((1, MID_ROWS, D), row), modspec, modspec,
                  pl.BlockSpec((1, D), const2), pl.BlockSpec((D, D), const2, pipeline_mode=pl.Buffered(1)),
                  pl.BlockSpec((D, 2 * D), const2, pipeline_mode=pl.Buffered(1)),
                  pl.BlockSpec((F_GROUP_DIM, 2 * F_GROUP_DIM), const2)],
        out_specs=[pl.BlockSpec((1, MID_ROWS, D), row),
                   pl.BlockSpec((1, 2, MID_ROWS, D), lambda b, i: (b, 0, i, 0)),
                   pl.BlockSpec((1, MID_ROWS, D), row)],
        out_shape=[jax.ShapeDtypeStruct((BATCH, SEQ, D), F32),
                   jax.ShapeDtypeStruct((BATCH, 2, SEQ, D), BF16),
                   jax.ShapeDtypeStruct((BATCH, SEQ, D), BF16)],
        compiler_params=pltpu.CompilerParams(
            dimension_semantics=("arbitrary", "arbitrary"), vmem_limit_bytes=VMEM_LIMIT),
        name="mid",
    )(x, og, mod0, mod1, norm_g.reshape(1, D), w_out.astype(BF16), w_f.astype(BF16),
      jnp.asarray(_CHANNEL_DFT).astype(BF16))


RADIX = 16
FBLK = RADIX * RADIX
STAGE_BLOCKS = 4
STAGE_STEPS = RADIX // STAGE_BLOCKS


def _stage_matrices():
    w = lambda e: np.exp(-2j * np.pi * (e % SEQ) / SEQ)
    i = np.arange(RADIX)
    eye = np.eye(RADIX)
    g = np.einsum("cd,kac->dkac", eye, w(256 * i[:, None, None] * i[None, :, None] + i[:, None, None] * i[None, None, :]))
    g = g.reshape(FBLK, FBLK)
    m1 = np.block([[g.real, g.imag], [g.imag, -g.real]])
    h = np.einsum("jk,qbk->qjbk", eye, w(256 * i[:, None, None] * i[None, :, None] + 16 * i[None, None, :] * i[None, :, None]))
    h = h.reshape(FBLK, FBLK)
    m2 = np.block([[h.real, -h.imag], [h.imag, h.real]])
    ph = w(i[None, None, :] * (16 * i[:, None, None] + 256 * i[None, :, None]))
    e = np.einsum("jk,bqc->bqjck", eye, ph).reshape(RADIX, FBLK, FBLK)
    m3 = np.concatenate([e.real, -e.imag], axis=2) / 64.0
    return m1.astype(np.float32), m2.astype(np.float32), m3.astype(np.float32)


_M1, _M2, _M3 = _stage_matrices()


def _fourier_kernel(ab_ref, m1_ref, m2_ref, m3_ref, sg_ref, x1_ref, mod_ref, wo_ref, fg_ref, o_ref, z_ref):
    s = pl.program_id(1)

    @pl.when(s < STAGE_STEPS)
    def _():
        for i in range(STAGE_BLOCKS):
            b = s * STAGE_BLOCKS + i
            rhs = ab_ref[0, :, :, i].reshape(2 * FBLK, D)
            y = jnp.dot(m1_ref[...], rhs, preferred_element_type=F32).astype(BF16)
            z_ref[:, :, pl.ds(b, 1)] = y.reshape(2, RADIX, 1, RADIX, D)

    @pl.when((s >= STAGE_STEPS) & (s < 2 * STAGE_STEPS))
    def _():
        for i in range(STAGE_BLOCKS):
            c = (s - STAGE_STEPS) * STAGE_BLOCKS + i
            rhs = z_ref[:, pl.ds(c, 1)].reshape(2 * FBLK, D)
            y = jnp.dot(m2_ref[...], rhs, preferred_element_type=F32).astype(BF16)
            z_ref[:, pl.ds(c, 1)] = y.reshape(2, 1, RADIX, RADIX, D)

    @pl.when(s >= 2 * STAGE_STEPS)
    def _():
        def stage3(i):
            kb = (s - 2 * STAGE_STEPS) * STAGE_BLOCKS + i
            rhs = z_ref[:, :, pl.ds(kb, 1)].reshape(2 * FBLK, D)
            return jnp.dot(m3_ref[i], rhs, preferred_element_type=F32)

        f_next = stage3(0)
        for i in range(STAGE_BLOCKS):
            f = f_next
            if i + 1 < STAGE_BLOCKS:
                f_next = stage3(i + 1)
            fg = (f * sg_ref[0, :, i].reshape(FBLK, D).astype(F32)).astype(BF16)
            x2 = x1_ref[0, :, i].reshape(FBLK, D) + mod_ref[0, 2:3, :] * jnp.dot(fg, wo_ref[...], preferred_element_type=F32)
            o_ref[0, :, i] = (_rms_rows(x2) * fg_ref[...]).reshape(RADIX, RADIX, D)


def _fourier(ab, sg, x1, mod1, w_out, final_g):
    digits = (RADIX, RADIX, RADIX)
    const2 = lambda b, s: (0, 0)
    last = lambda s: jnp.clip(s - 2 * STAGE_STEPS, 0, STAGE_STEPS - 1)
    rows3 = pl.BlockSpec((1, RADIX, STAGE_BLOCKS, RADIX, D), lambda b, s: (b, 0, last(s), 0, 0))
    out = pl.pallas_call(
        _fourier_kernel,
        grid=(BATCH, 3 * STAGE_STEPS),
        in_specs=[pl.BlockSpec((1, 2, RADIX, STAGE_BLOCKS, RADIX, D),
                               lambda b, s: (b, 0, 0, jnp.minimum(s, STAGE_STEPS - 1), 0, 0)),
                  pl.BlockSpec((2 * FBLK, 2 * FBLK), const2),
                  pl.BlockSpec((2 * FBLK, 2 * FBLK), const2),
                  pl.BlockSpec((STAGE_BLOCKS, FBLK, 2 * FBLK), lambda b, s: (last(s), 0, 0)),
                  rows3, rows3,
                  pl.BlockSpec((1, 3, D), lambda b, s: (b, 0, 0)),
                  pl.BlockSpec((D, D), const2), pl.BlockSpec((1, D), const2)],
        out_specs=rows3,
        out_shape=jax.ShapeDtypeStruct((BATCH,) + digits + (D,), F32),
        scratch_shapes=[pltpu.VMEM((2,) + digits + (D,), BF16)],
        compiler_params=pltpu.CompilerParams(
            dimension_semantics=("arbitrary", "arbitrary"), vmem_limit_bytes=VMEM_LIMIT),
        name="fourier",
    )(ab.reshape((BATCH, 2) + digits + (D,)), jnp.asarray(_M1).astype(BF16), jnp.asarray(_M2).astype(BF16),
      jnp.asarray(_M3).astype(BF16), sg.reshape((BATCH,) + digits + (D,)), x1.reshape((BATCH,) + digits + (D,)),
      mod1, w_out.astype(BF16), final_g.reshape(1, D))
    return out.reshape(BATCH, SEQ, D)


def kernel(x, c, ctx, c_ctx, ada_w, ada_b, norm_g, attn_in_w, attn_qn_g, attn_kn_g, lam_q1, lam_k1, lam_q2, lam_k2,
           attn_subln_g, attn_out_w, fourier_in_w, fourier_out_w, final_g):
    assert x.shape == (BATCH, SEQ, D) and ctx.shape == (BATCH, CTX, D)
    assert ada_w.shape[0] == 2 and attn_in_w.shape == (1, D, IN_COLS)
    cond = jnp.concatenate([c, c_ctx[None, :], jnp.zeros((16 - BATCH - 1, D), F32)], axis=0)
    mods = _ada(cond, ada_w, ada_b)
    mod0 = mods[0].reshape(16, 3, D)
    mod1 = mods[1].reshape(16, 3, D)
    lam_init = 0.8 - 0.6 * math.exp(-0.3 * 0)

    k_ctx, vt_ctx, k, vt, q, gate = _inproj(x, ctx, mod0, norm_g[0], attn_in_w[0], attn_qn_g[0], attn_kn_g[0])
    og = _attention(q, k_ctx, vt_ctx, k, vt, gate, lam_q1[0], lam_k1[0], lam_q2[0], lam_k2[0], attn_subln_g[0], lam_init)
    x1, ab, sg = _mid(x, og, mod0, mod1, norm_g[1], attn_out_w[0], fourier_in_w[0])
    return _fourier(ab, sg, x1, mod1, fourier_out_w[0], final_g)
```
